```python
import math
import jax, jax.numpy as jnp
from jax import lax
import numpy as np

D_MODEL = 1024
BATCH = 4
SEQ = 8192
DEPTH = 2

N_EVEN = (DEPTH + 1) // 2
N_ODD = DEPTH // 2

POOL_WIDTH = D_MODEL // 2
POOL_WINDOWS = (2, 4, 8, 16)
N_POOL_GROUPS = len(POOL_WINDOWS)
POOL_GROUP = POOL_WIDTH // N_POOL_GROUPS

ATTN_WIDTH = D_MODEL - POOL_WIDTH
DA_HEAD_V = 128
DA_HEADS = ATTN_WIDTH // DA_HEAD_V
DA_HEAD_QK = DA_HEAD_V // 2
ROT_DIM = DA_HEAD_QK // 4
ROPE_THETA = 500000.0
Q_BLOCK = 128

IN_WIDTH = POOL_WIDTH + 3 * ATTN_WIDTH

CONV_WIDTH = D_MODEL
CONV_KERNEL = 31

D_FF = 4 * D_MODEL

RMS_EPS = 1e-6
LN_EPS = 1e-5
SUBLN_EPS = 1e-5

kernel_name = "hybrid_pool_diffattn_conformer_trunk"


def rmsnorm(x, g, eps=RMS_EPS):
    xf = x.astype(jnp.float32)
    y = xf * lax.rsqrt(jnp.mean(xf * xf, axis=-1, keepdims=True) + eps)
    return (y * g.astype(jnp.float32)).astype(x.dtype)


def layernorm(x, g, b, eps=LN_EPS):
    xf = x.astype(jnp.float32)
    mu = jnp.mean(xf, axis=-1, keepdims=True)
    var = jnp.mean(jnp.square(xf - mu), axis=-1, keepdims=True)
    y = (xf - mu) * lax.rsqrt(var + eps)
    return (y * g.astype(jnp.float32) + b.astype(jnp.float32)).astype(x.dtype)


def rope_tables(S):
    pos = jnp.arange(S, dtype=jnp.float32)
    inv_freq = ROPE_THETA ** (-jnp.arange(0, ROT_DIM, 2, dtype=jnp.float32) / ROT_DIM)
    ang = pos[:, None] * inv_freq[None, :]
    return jnp.cos(ang), jnp.sin(ang)


def rope_partial(x, cos, sin):
    half = ROT_DIM // 2
    xr = x[..., :ROT_DIM].astype(jnp.float32)
    x1, x2 = xr[..., :half], xr[..., half:]
    c = cos[None, :, None, None, :]
    s = sin[None, :, None, None, :]
    rot = jnp.concatenate([x1 * c - x2 * s, x2 * c + x1 * s], axis=-1).astype(x.dtype)
    return jnp.concatenate([rot, x[..., ROT_DIM:]], axis=-1)


def multiscale_pool(u):
    B, S, _ = u.shape
    uf = u.astype(jnp.float32).reshape(B, S, N_POOL_GROUPS, POOL_GROUP)
    cs = jnp.cumsum(uf, axis=1)
    t = jnp.arange(S)
    outs = []
    for g, w in enumerate(POOL_WINDOWS):
        c = cs[:, :, g]
        prev = jnp.pad(c, ((0, 0), (w, 0), (0, 0)))[:, :S]
        cnt = jnp.minimum(t + 1, w).astype(jnp.float32)[None, :, None]
        outs.append((c - prev) / cnt - uf[:, :, g])
    return jnp.stack(outs, axis=2).astype(u.dtype)


def diff_attention(q, k, v, lam):
    B, S = q.shape[:2]
    nb = S // Q_BLOCK
    scale = DA_HEAD_QK ** -0.5
    qb = q.reshape(B, nb, Q_BLOCK, DA_HEADS, 2, DA_HEAD_QK).transpose(1, 0, 2, 3, 4, 5)
    k_pos = jnp.arange(S)

    def one_block(args):
        i, qi = args
        s = jnp.einsum('bqhcd,bkhcd->bhcqk', qi, k,
                       preferred_element_type=jnp.float32) * scale
        q_pos = i * Q_BLOCK + jnp.arange(Q_BLOCK)
        mask = k_pos[None, :] <= q_pos[:, None]
        s = jnp.where(mask, s, -jnp.inf)
        p = jax.nn.softmax(s, axis=-1)
        w = p[:, :, 0] - lam * p[:, :, 1]
        return jnp.einsum('bhqk,bkhe->bqhe', w.astype(v.dtype), v)

    out = lax.map(one_block, (jnp.arange(nb), qb))
    return out.transpose(1, 0, 2, 3, 4).reshape(B, S, DA_HEADS, DA_HEAD_V)


def pool_diff_mixer(h, layer_idx, cos, sin, w_in, pool_w, pool_scale,
                    lam_q1, lam_k1, lam_q2, lam_k2, subln, w_out):
    B, S, _ = h.shape
    z = h @ w_in
    u = z[..., :POOL_WIDTH]
    q = z[..., POOL_WIDTH:POOL_WIDTH + ATTN_WIDTH].reshape(B, S, DA_HEADS, 2, DA_HEAD_QK)
    k = z[..., POOL_WIDTH + ATTN_WIDTH:POOL_WIDTH + 2 * ATTN_WIDTH].reshape(
        B, S, DA_HEADS, 2, DA_HEAD_QK)
    v = z[..., POOL_WIDTH + 2 * ATTN_WIDTH:].reshape(B, S, DA_HEADS, DA_HEAD_V)

    pooled = multiscale_pool(u)
    a_out = jnp.einsum('bsgc,gcd->bsgd', pooled, pool_w).reshape(B, S, POOL_WIDTH)
    a_out = a_out * pool_scale

    q = rope_partial(q, cos, sin)
    k = rope_partial(k, cos, sin)
    lambda_init = 0.8 - 0.6 * math.exp(-0.3 * layer_idx)
    lam = (jnp.exp(jnp.sum(lam_q1.astype(jnp.float32) * lam_k1.astype(jnp.float32)))
           - jnp.exp(jnp.sum(lam_q2.astype(jnp.float32) * lam_k2.astype(jnp.float32)))
           + lambda_init)
    o = diff_attention(q, k, v, lam)
    o = rmsnorm(o, subln, SUBLN_EPS) * (1.0 - lambda_init)
    b_out = o.reshape(B, S, ATTN_WIDTH)

    return jnp.concatenate([a_out, b_out.astype(a_out.dtype)], axis=-1) @ w_out


def conformer_conv(h, pw1_w, pw1_b, dw_w, dw_b, ln_g, ln_b, pw2_w, pw2_b):
    a = h @ pw1_w + pw1_b
    g = a[..., :CONV_WIDTH] * jax.nn.sigmoid(a[..., CONV_WIDTH:])
    y = lax.conv_general_dilated(
        g, dw_w[:, None, :].astype(g.dtype), window_strides=(1,),
        padding=((CONV_KERNEL - 1, 0),),
        dimension_numbers=('NWC', 'WIO', 'NWC'),
        feature_group_count=CONV_WIDTH) + dw_b
    y = jax.nn.silu(layernorm(y, ln_g, ln_b))
    return y @ pw2_w + pw2_b


def sq_relu_mlp(h, w_up, w_down):
    return jnp.square(jax.nn.relu(h @ w_up)) @ w_down


def setup_inputs(seed: int = 0) -> dict:
    key = jax.random.key(seed)
    ks = iter(jax.random.split(key, 32))

    def nrm(shape, scale):
        return jax.random.normal(next(ks), shape, jnp.float32) * scale

    def gain(shape):
        return 1.0 + nrm(shape, 0.05)

    return {
        "x": nrm((BATCH, SEQ, D_MODEL), 1.0),
        "mix_norm": gain((DEPTH, D_MODEL)),
        "mlp_norm": gain((DEPTH, D_MODEL)),
        "w_up": nrm((DEPTH, D_MODEL, D_FF), D_MODEL ** -0.5),
        "w_down": nrm((DEPTH, D_FF, D_MODEL), D_FF ** -0.5),
        "final_norm": gain((D_MODEL,)),
        "w_in": nrm((N_EVEN, D_MODEL, IN_WIDTH), D_MODEL ** -0.5),
        "pool_w": nrm((N_EVEN, N_POOL_GROUPS, POOL_GROUP, POOL_GROUP), POOL_GROUP ** -0.5),
        "pool_scale": gain((N_EVEN, POOL_WIDTH)),
        "lam_q1": nrm((N_EVEN, DA_HEAD_QK), 0.1),
        "lam_k1": nrm((N_EVEN, DA_HEAD_QK), 0.1),
        "lam_q2": nrm((N_EVEN, DA_HEAD_QK), 0.1),
        "lam_k2": nrm((N_EVEN, DA_HEAD_QK), 0.1),
        "subln": gain((N_EVEN, DA_HEAD_V)),
        "w_out": nrm((N_EVEN, D_MODEL, D_MODEL), D_MODEL ** -0.5),
        "conv_pw1_w": nrm((N_ODD, D_MODEL, 2 * CONV_WIDTH), D_MODEL ** -0.5),
        "conv_pw1_b": nrm((N_ODD, 2 * CONV_WIDTH), 0.02),
        "conv_dw_w": nrm((N_ODD, CONV_KERNEL, CONV_WIDTH), CONV_KERNEL ** -0.5),
        "conv_dw_b": nrm((N_ODD, CONV_WIDTH), 0.02),
        "conv_ln_g": gain((N_ODD, CONV_WIDTH)),
        "conv_ln_b": nrm((N_ODD, CONV_WIDTH), 0.02),
        "conv_pw2_w": nrm((N_ODD, CONV_WIDTH, D_MODEL), CONV_WIDTH ** -0.5),
        "conv_pw2_b": nrm((N_ODD, D_MODEL), 0.02),
    }


def reference(x, mix_norm, mlp_norm, w_up, w_down, final_norm,
              w_in, pool_w, pool_scale, lam_q1, lam_k1, lam_q2, lam_k2, subln, w_out,
              conv_pw1_w, conv_pw1_b, conv_dw_w, conv_dw_b, conv_ln_g, conv_ln_b,
              conv_pw2_w, conv_pw2_b):
    S = x.shape[1]
    cos, sin = rope_tables(S)
    h = x
    for l in range(DEPTH):
        j = l // 2
        hn = rmsnorm(h, mix_norm[l])
        if l % 2 == 0:
            mix = pool_diff_mixer(hn, l, cos, sin, w_in[j], pool_w[j], pool_scale[j],
                                  lam_q1[j], lam_k1[j], lam_q2[j], lam_k2[j],
                                  subln[j], w_out[j])
        else:
            mix = conformer_conv(hn, conv_pw1_w[j], conv_pw1_b[j], conv_dw_w[j],
                                 conv_dw_b[j], conv_ln_g[j], conv_ln_b[j],
                                 conv_pw2_w[j], conv_pw2_b[j])
        h = h + mix.astype(h.dtype)
        h = h + sq_relu_mlp(rmsnorm(h, mlp_norm[l]), w_up[l], w_down[l]).astype(h.dtype)
    return rmsnorm(h, final_norm)
```

```python
import functools
import math

import jax
import jax.numpy as jnp
from jax import lax
from jax.experimental import pallas as pl
from jax.experimental.pallas import tpu as pltpu

F32 = jnp.float32
BF16 = jnp.bfloat16

D_MODEL = 1024
POOL_WIDTH = 512
POOL_WINDOWS = (2, 4, 8, 16)
POOL_GROUP = 128
ATTN_WIDTH = 512
DA_HEAD_V = 128
DA_HEADS = 4
DA_HEAD_QK = 64
ROT_DIM = 16
ROPE_THETA = 500000.0
IN_WIDTH = 2048
CONV_WIDTH = 1024
CONV_KERNEL = 31
D_FF = 4096
RMS_EPS = 1e-6
LN_EPS = 1e-5
SUBLN_EPS = 1e-5

V7X_VMEM_LIMIT_BYTES = 56 * 1024 * 1024
SUBLANES = 8
LANES = 128

TOK_TILE = 512
ATT_TILE = 512
FF_CHUNK = 1024
POOL_HALO = 16
CONV_HALO = 32
CONV_ROWS = 16


def _rms(x, g, eps):
    ms = jnp.mean(x * x, axis=-1, keepdims=True)
    return x * lax.rsqrt(ms + eps) * g


def _const_spec(shape):
    nd = len(shape)
    return pl.BlockSpec(shape, lambda *_: (0,) * nd, pipeline_mode=pl.Buffered(1))


def _mlp(h, g, wup_ref, wdn_ref):
    hn = _rms(h, g, RMS_EPS).astype(BF16)
    acc = None
    for c in range(D_FF // FF_CHUNK):
        cols = slice(c * FF_CHUNK, (c + 1) * FF_CHUNK)
        up = jnp.dot(hn, wup_ref[:, cols], preferred_element_type=F32)
        r = jnp.maximum(up, 0.0)
        d = jnp.dot((r * r).astype(BF16), wdn_ref[cols, :], preferred_element_type=F32)
        acc = d if acc is None else acc + d
    return acc


def _front0_kernel(x_ref, g_ref, win_ref, pw_ref, ps_ref, rc_ref, rs1_ref, rs2_ref,
                   a_ref, q_ref, k_ref, v_ref, ubuf):
    t = pl.program_id(1)
    T = x_ref.shape[0]
    hn = _rms(x_ref[...], g_ref[...], RMS_EPS).astype(BF16)
    z = jnp.dot(hn, win_ref[...], preferred_element_type=F32)

    @pl.when(t == 0)
    def _():
        ubuf[0:POOL_HALO, :] = jnp.zeros((POOL_HALO, POOL_WIDTH), F32)

    ubuf[POOL_HALO:POOL_HALO + T, :] = z[:, :POOL_WIDTH]
    pos = t * T + lax.broadcasted_iota(jnp.int32, (T, 1), 0)
    for g, w in enumerate(POOL_WINDOWS):
        lanes = slice(g * POOL_GROUP, (g + 1) * POOL_GROUP)
        ug = ubuf[POOL_HALO:POOL_HALO + T, lanes]
        acc = ug
        for j in range(1, w):
            acc = acc + ubuf[POOL_HALO - j:POOL_HALO - j + T, lanes]
        cnt = jnp.minimum(pos + 1, w).astype(F32)
        pooled = acc / cnt - ug
        ag = jnp.dot(pooled.astype(BF16), pw_ref[g], preferred_element_type=F32)
        a_ref[:, lanes] = (ag * ps_ref[:, lanes]).astype(BF16)
    ubuf[0:POOL_HALO, :] = ubuf[T:T + POOL_HALO, :]

    rc = rc_ref[...]
    rs1 = rs1_ref[...]
    rs2 = rs2_ref[...]
    scale = DA_HEAD_QK ** -0.5
    for h in range(DA_HEADS):
        lanes = slice(h * LANES, (h + 1) * LANES)
        for off, dst, sc in ((POOL_WIDTH, q_ref, scale), (POOL_WIDTH + ATTN_WIDTH, k_ref, None)):
            xh = z[:, off + h * LANES: off + (h + 1) * LANES]
            rot = (xh * rc + pltpu.roll(xh, LANES - ROT_DIM // 2, 1) * rs1
                   + pltpu.roll(xh, ROT_DIM // 2, 1) * rs2)
            if sc is not None:
                rot = rot * sc
            dst[:, lanes] = rot.astype(BF16)
    v_ref[...] = z[:, POOL_WIDTH + 2 * ATTN_WIDTH:].astype(BF16)


def _front0(x, g, w_in, pool_w, pool_scale, rc, rs1, rs2):
    B, S, D = x.shape
    T = TOK_TILE
    out = jax.ShapeDtypeStruct((B, S, ATTN_WIDTH), BF16)
    tok = lambda w: pl.BlockSpec((None, T, w), lambda b, t: (b, t, 0))
    rope = pl.BlockSpec((T, LANES), lambda b, t: (t, 0))
    return pl.pallas_call(
        _front0_kernel,
        grid=(B, S // T),
        in_specs=[tok(D), _const_spec((1, D)), _const_spec((D, IN_WIDTH)),
                  _const_spec((len(POOL_WINDOWS), POOL_GROUP, POOL_GROUP)),
                  _const_spec((1, POOL_WIDTH)), rope, rope, rope],
        out_specs=[tok(POOL_WIDTH), tok(ATTN_WIDTH), tok(ATTN_WIDTH), tok(ATTN_WIDTH)],
        out_shape=[out, out, out, out],
        scratch_shapes=[pltpu.VMEM((POOL_HALO + T, POOL_WIDTH), F32)],
        compiler_params=pltpu.CompilerParams(
            dimension_semantics=("arbitrary", "arbitrary"),
            vmem_limit_bytes=V7X_VMEM_LIMIT_BYTES),
        name="front0",
    )(x, g, w_in, pool_w, pool_scale, rc, rs1, rs2)


def _attn_kernel(lam_ref, subln_ref, q_ref, k_ref, v_ref, o_ref,
                 m1, l1, acc1, m2, l2, acc2, *, lambda_init):
    i = pl.program_id(2)
    TQ = q_ref.shape[0]
    TK = TQ
    q = q_ref[...]
    lane = lax.broadcasted_iota(jnp.int32, q.shape, 1)
    zero = jnp.zeros_like(q)
    qs = (jnp.where(lane < DA_HEAD_QK, q, zero), jnp.where(lane >= DA_HEAD_QK, q, zero))
    stats = ((m1, l1, acc1), (m2, l2, acc2))
    for m, l, acc in stats:
        m[...] = jnp.full(m.shape, -jnp.inf, F32)
        l[...] = jnp.zeros(l.shape, F32)
        acc[...] = jnp.zeros(acc.shape, F32)

    def step(j, masked):
        start = pl.multiple_of(j * TK, TK)
        kt = k_ref[pl.ds(start, TK), :]
        vt = v_ref[pl.ds(start, TK), :]
        for qc, (m, l, acc) in zip(qs, stats):
            s = lax.dot_general(qc, kt, (((1,), (1,)), ((), ())),
                                preferred_element_type=F32)
            if masked:
                row = lax.broadcasted_iota(jnp.int32, s.shape, 0)
                col = lax.broadcasted_iota(jnp.int32, s.shape, 1)
                s = jnp.where(col <= row, s, -jnp.inf)
            m_old = m[...]
            m_new = jnp.maximum(m_old, jnp.max(s, axis=-1, keepdims=True))
            alpha = jnp.exp(m_old - m_new)
            p = jnp.exp(s - m_new)
            l[...] = alpha * l[...] + jnp.sum(p, axis=-1, keepdims=True)
            acc[...] = alpha * acc[...] + jnp.dot(p.astype(BF16), vt,
                                                  preferred_element_type=F32)
            m[...] = m_new

    def body(j, carry):
        step(j, False)
        return carry

    lax.fori_loop(0, i, body, 0)
    step(i, True)

    lamv = lam_ref[...]
    lam = (jnp.exp(jnp.sum(lamv[0:1] * lamv[1:2], axis=-1, keepdims=True))
           - jnp.exp(jnp.sum(lamv[2:3] * lamv[3:4], axis=-1, keepdims=True))
           + lambda_init)
    o = acc1[...] / l1[...] - lam * (acc2[...] / l2[...])
    o = _rms(o, subln_ref[...], SUBLN_EPS) * (1.0 - lambda_init)
    o_ref[...] = o.astype(o_ref.dtype)


def _attn(q, k, v, lamv, subln, lambda_init):
    B, S, _ = q.shape
    TQ = ATT_TILE
    stat = pltpu.VMEM((TQ, 1), F32)
    accs = pltpu.VMEM((TQ, DA_HEAD_V), F32)
    return pl.pallas_call(
        functools.partial(_attn_kernel, lambda_init=lambda_init),
        grid=(B, DA_HEADS, S // TQ),
        in_specs=[_const_spec((4, DA_HEAD_QK)), _const_spec((1, DA_HEAD_V)),
                  pl.BlockSpec((None, TQ, LANES), lambda b, h, i: (b, i, h)),
                  pl.BlockSpec((None, S, LANES), lambda b, h, i: (b, 0, h)),
                  pl.BlockSpec((None, S, LANES), lambda b, h, i: (b, 0, h))],
        out_specs=pl.BlockSpec((None, TQ, LANES), lambda b, h, i: (b, i, h)),
        out_shape=jax.ShapeDtypeStruct((B, S, ATTN_WIDTH), BF16),
        scratch_shapes=[stat, stat, accs, stat, stat, accs],
        compiler_params=pltpu.CompilerParams(
            dimension_semantics=("arbitrary", "arbitrary", "arbitrary"),
            vmem_limit_bytes=V7X_VMEM_LIMIT_BYTES),
        name="attn",
    )(lamv, subln, q, k, v)


def _mid0_kernel(x_ref, a_ref, b_ref, wo_ref, g_ref, wup_ref, wdn_ref, o_ref):
    mix = (jnp.dot(a_ref[...], wo_ref[0:POOL_WIDTH, :], preferred_element_type=F32)
           + jnp.dot(b_ref[...], wo_ref[POOL_WIDTH:, :], preferred_element_type=F32))
    h = x_ref[...] + mix
    o_ref[...] = h + _mlp(h, g_ref[...], wup_ref, wdn_ref)


def _mid0(x, a, b, w_out, g, w_up, w_down):
    N, D = x.shape
    T = TOK_TILE
    tok = lambda w: pl.BlockSpec((T, w), lambda t: (t, 0))
    return pl.pallas_call(
        _mid0_kernel,
        grid=(N // T,),
        in_specs=[tok(D), tok(POOL_WIDTH), tok(ATTN_WIDTH), _const_spec((D, D)),
                  _const_spec((1, D)), _const_spec((D, D_FF)), _const_spec((D_FF, D))],
        out_specs=tok(D),
        out_shape=jax.ShapeDtypeStruct((N, D), F32),
        compiler_params=pltpu.CompilerParams(
            dimension_semantics=("arbitrary",),
            vmem_limit_bytes=V7X_VMEM_LIMIT_BYTES),
        name="mid0",
    )(x, a, b, w_out, g, w_up, w_down)


def _layer1_kernel(x_ref, gmix_ref, pw1w_ref, pw1b_ref, dww_ref, dwb_ref, lng_ref, lnb_ref,
                   pw2w_ref, pw2b_ref, gmlp_ref, wup_ref, wdn_ref, gfin_ref, o_ref,
                   gbuf, ybuf):
    t = pl.program_id(1)
    T = x_ref.shape[0]
    x = x_ref[...]
    hn = _rms(x, gmix_ref[...], RMS_EPS).astype(BF16)
    a = jnp.dot(hn, pw1w_ref[...], preferred_element_type=F32) + pw1b_ref[...]
    glu = a[:, :CONV_WIDTH] * jax.nn.sigmoid(a[:, CONV_WIDTH:])

    @pl.when(t == 0)
    def _():
        gbuf[0:CONV_HALO, :] = jnp.zeros((CONV_HALO, CONV_WIDTH), F32)

    gbuf[CONV_HALO:CONV_HALO + T, :] = glu

    lead = CONV_HALO - (CONV_KERNEL - 1)

    def conv_chunk(r, carry):
        base = pl.multiple_of(r * CONV_ROWS, CONV_ROWS)
        acc = jnp.zeros((CONV_ROWS, CONV_WIDTH), F32)
        win = gbuf[pl.ds(base, CONV_ROWS + CONV_HALO), :]
        for k in range(CONV_KERNEL):
            acc = acc + dww_ref[k:k + 1, :] * win[lead + k:lead + k + CONV_ROWS, :]
        ybuf[pl.ds(base, CONV_ROWS), :] = acc
        return carry

    lax.fori_loop(0, T // CONV_ROWS, conv_chunk, 0)
    gbuf[0:CONV_HALO, :] = gbuf[T:T + CONV_HALO, :]

    y = ybuf[...] + dwb_ref[...]
    mu = jnp.mean(y, axis=-1, keepdims=True)
    yc = y - mu
    var = jnp.mean(yc * yc, axis=-1, keepdims=True)
    y = yc * lax.rsqrt(var + LN_EPS) * lng_ref[...] + lnb_ref[...]
    y = y * jax.nn.sigmoid(y)
    c = jnp.dot(y.astype(BF16), pw2w_ref[...], preferred_element_type=F32) + pw2b_ref[...]
    h = x + c
    h = h + _mlp(h, gmlp_ref[...], wup_ref, wdn_ref)
    o_ref[...] = _rms(h, gfin_ref[...], RMS_EPS)


def _layer1(x, gmix, pw1w, pw1b, dww, dwb, lng, lnb, pw2w, pw2b, gmlp, w_up, w_down, gfin):
    B, S, D = x.shape
    T = TOK_TILE
    tok = pl.BlockSpec((None, T, D), lambda b, t: (b, t, 0))
    vec = lambda w: _const_spec((1, w))
    return pl.pallas_call(
        _layer1_kernel,
        grid=(B, S // T),
        in_specs=[tok, vec(D), _const_spec((D, 2 * CONV_WIDTH)), vec(2 * CONV_WIDTH),
                  _const_spec((CONV_KERNEL, CONV_WIDTH)), vec(CONV_WIDTH), vec(CONV_WIDTH),
                  vec(CONV_WIDTH), _const_spec((CONV_WIDTH, D)), vec(D), vec(D),
                  _const_spec((D, D_FF)), _const_spec((D_FF, D)), vec(D)],
        out_specs=tok,
        out_shape=jax.ShapeDtypeStruct((B, S, D), F32),
        scratch_shapes=[pltpu.VMEM((CONV_HALO + T, CONV_WIDTH), F32),
                        pltpu.VMEM((T, CONV_WIDTH), F32)],
        compiler_params=pltpu.CompilerParams(
            dimension_semantics=("arbitrary", "arbitrary"),
            vmem_limit_bytes=V7X_VMEM_LIMIT_BYTES),
        name="layer1",
    )(x, gmix, pw1w, pw1b, dww, dwb, lng, lnb, pw2w, pw2b, gmlp, w_up, w_down, gfin)


def _rope_lane_tables(S):
    half = ROT_DIM // 2
    pos = jnp.arange(S, dtype=F32)
    inv_freq = ROPE_THETA ** (-jnp.arange(0, ROT_DIM, 2, dtype=F32) / ROT_DIM)
    ang = pos[:, None] * inv_freq[None, :]
    cos, sin = jnp.cos(ang), jnp.sin(ang)
    rest = DA_HEAD_QK - ROT_DIM
    one = jnp.ones((S, rest), F32)
    zero = jnp.zeros((S, rest), F32)
    zh = jnp.zeros((S, half), F32)
    rc = jnp.concatenate([cos, cos, one], axis=-1)
    rs1 = jnp.concatenate([-sin, zh, zero], axis=-1)
    rs2 = jnp.concatenate([zh, sin, zero], axis=-1)
    rep = LANES // DA_HEAD_QK
    return tuple(jnp.tile(tbl, (1, rep)) for tbl in (rc, rs1, rs2))


def kernel(x, mix_norm, mlp_norm, w_up, w_down, final_norm, w_in, pool_w, pool_scale,
           lam_q1, lam_k1, lam_q2, lam_k2, subln, w_out, conv_pw1_w, conv_pw1_b,
           conv_dw_w, conv_dw_b, conv_ln_g, conv_ln_b, conv_pw2_w, conv_pw2_b):
    B, S, D = x.shape
    row = lambda v: v.reshape(1, -1)
    rc, rs1, rs2 = _rope_lane_tables(S)

    a, q, k, v = _front0(x, row(mix_norm[0]), w_in[0].astype(BF16), pool_w[0].astype(BF16),
                         row(pool_scale[0]), rc, rs1, rs2)
    lamv = jnp.stack([lam_q1[0], lam_k1[0], lam_q2[0], lam_k2[0]]).astype(F32)
    lambda_init = 0.8 - 0.6 * math.exp(-0.3 * 0)
    b = _attn(q, k, v, lamv, row(subln[0]), lambda_init)
    h = _mid0(x.reshape(B * S, D), a.reshape(B * S, POOL_WIDTH), b.reshape(B * S, ATTN_WIDTH),
              w_out[0].astype(BF16), row(mlp_norm[0]), w_up[0].astype(BF16),
              w_down[0].astype(BF16))

    return _layer1(h.reshape(B, S, D), row(mix_norm[1]), conv_pw1_w[0].astype(BF16),
                   row(conv_pw1_b[0]), conv_dw_w[0], row(conv_dw_b[0]), row(conv_ln_g[0]),
                   row(conv_ln_b[0]), conv_pw2_w[0].astype(BF16), row(conv_pw2_b[0]),
                   row(mlp_norm[1]), w_up[1].astype(BF16), w_down[1].astype(BF16),
                   row(final_norm))
```

```python
import functools
import math

import jax
import jax.numpy as jnp
from jax import lax
from jax.experimental import pallas as pl
from jax.experimental.pallas import tpu as pltpu

F32 = jnp.float32
BF16 = jnp.bfloat16

D_MODEL = 1024
POOL_WIDTH = 512
POOL_WINDOWS = (2, 4, 8, 16)
POOL_GROUP = 128
ATTN_WIDTH = 512
DA_HEAD_V = 128
DA_HEADS = 4
DA_HEAD_QK = 64
ROT_DIM = 16
ROPE_THETA = 500000.0
IN_WIDTH = 2048
CONV_WIDTH = 1024
CONV_KERNEL = 31
D_FF = 4096
RMS_EPS = 1e-6
LN_EPS = 1e-5
SUBLN_EPS = 1e-5

V7X_VMEM_LIMIT_BYTES = 56 * 1024 * 1024
SUBLANES = 8
LANES = 128

TOK_TILE = 512
ATT_TILE = 512
FF_CHUNK = 1024
POOL_HALO = 16
CONV_HALO = 32
CONV_ROWS = 16


def _rms(x, g, eps):
    ms = jnp.mean(x * x, axis=-1, keepdims=True)
    return x * lax.rsqrt(ms + eps) * g


def _const_spec(shape):
    nd = len(shape)
    return pl.BlockSpec(shape, lambda *_: (0,) * nd, pipeline_mode=pl.Buffered(1))


def _mlp(h, g, wup_ref, wdn_ref):
    hn = _rms(h, g, RMS_EPS).astype(BF16)
    acc = None
    for c in range(D_FF // FF_CHUNK):
        cols = slice(c * FF_CHUNK, (c + 1) * FF_CHUNK)
        up = jnp.dot(hn, wup_ref[:, cols], preferred_element_type=F32)
        r = jnp.maximum(up, 0.0)
        d = jnp.dot((r * r).astype(BF16), wdn_ref[cols, :], preferred_element_type=F32)
        acc = d if acc is None else acc + d
    return acc


def _front0_kernel(x_ref, g_ref, win_ref, pw_ref, ps_ref, rc_ref, rs1_ref, rs2_ref,
                   a_ref, qT_ref, k_ref, vT_ref, ubuf):
    t = pl.program_id(1)
    T = x_ref.shape[0]
    hn = _rms(x_ref[...], g_ref[...], RMS_EPS).astype(BF16)
    z = jnp.dot(hn, win_ref[...], preferred_element_type=F32)

    @pl.when(t == 0)
    def _():
        ubuf[0:POOL_HALO, :] = jnp.zeros((POOL_HALO, POOL_WIDTH), F32)

    ubuf[POOL_HALO:POOL_HALO + T, :] = z[:, :POOL_WIDTH]
    pos = t * T + lax.broadcasted_iota(jnp.int32, (T, 1), 0)
    for g, w in enumerate(POOL_WINDOWS):
        lanes = slice(g * POOL_GROUP, (g + 1) * POOL_GROUP)
        ug = ubuf[POOL_HALO:POOL_HALO + T, lanes]
        acc = ug
        for j in range(1, w):
            acc = acc + ubuf[POOL_HALO - j:POOL_HALO - j + T, lanes]
        cnt = jnp.minimum(pos + 1, w).astype(F32)
        pooled = acc / cnt - ug
        ag = jnp.dot(pooled.astype(BF16), pw_ref[g], preferred_element_type=F32)
        a_ref[:, lanes] = (ag * ps_ref[:, lanes]).astype(BF16)
    ubuf[0:POOL_HALO, :] = ubuf[T:T + POOL_HALO, :]

    rc = rc_ref[...]
    rs1 = rs1_ref[...]
    rs2 = rs2_ref[...]
    scale = DA_HEAD_QK ** -0.5 * math.log2(math.e)
    def rope(xh):
        return (xh * rc + pltpu.roll(xh, LANES - ROT_DIM // 2, 1) * rs1
                + pltpu.roll(xh, ROT_DIM // 2, 1) * rs2)

    for h in range(DA_HEADS):
        lanes = slice(h * LANES, (h + 1) * LANES)
        qh = rope(z[:, POOL_WIDTH + h * LANES: POOL_WIDTH + (h + 1) * LANES]) * scale
        qT_ref[lanes, :] = qh.T.astype(BF16)
        off = POOL_WIDTH + ATTN_WIDTH
        k_ref[:, lanes] = rope(z[:, off + h * LANES: off + (h + 1) * LANES]).astype(BF16)
        off = POOL_WIDTH + 2 * ATTN_WIDTH
        vT_ref[lanes, :] = z[:, off + h * LANES: off + (h + 1) * LANES].T.astype(BF16)


def _front0(x, g, w_in, pool_w, pool_scale, rc, rs1, rs2):
    B, S, D = x.shape
    T = TOK_TILE
    out = jax.ShapeDtypeStruct((B, S, ATTN_WIDTH), BF16)
    outT = jax.ShapeDtypeStruct((B, S // T, ATTN_WIDTH, T), BF16)
    tok = lambda w: pl.BlockSpec((None, T, w), lambda b, t: (b, t, 0))
    tokT = pl.BlockSpec((None, None, ATTN_WIDTH, T), lambda b, t: (b, t, 0, 0))
    rope = pl.BlockSpec((T, LANES), lambda b, t: (t, 0))
    return pl.pallas_call(
        _front0_kernel,
        grid=(B, S // T),
        in_specs=[tok(D), _const_spec((1, D)), _const_spec((D, IN_WIDTH)),
                  _const_spec((len(POOL_WINDOWS), POOL_GROUP, POOL_GROUP)),
                  _const_spec((1, POOL_WIDTH)), rope, rope, rope],
        out_specs=[tok(POOL_WIDTH), tokT, tok(ATTN_WIDTH), tokT],
        out_shape=[out, outT, out, outT],
        scratch_shapes=[pltpu.VMEM((POOL_HALO + T, POOL_WIDTH), F32)],
        compiler_params=pltpu.CompilerParams(
            dimension_semantics=("arbitrary", "arbitrary"),
            vmem_limit_bytes=V7X_VMEM_LIMIT_BYTES),
        name="front0",
    )(x, g, w_in, pool_w, pool_scale, rc, rs1, rs2)


def _attn_kernel(lam_ref, subln_ref, qT_ref, k_ref, vT_ref, o_ref, acc1, acc2, s1buf, s2buf,
                 qbuf, *, lambda_init):
    i = pl.program_id(2)
    TQ = qT_ref.shape[1]
    qT = qT_ref[...]
    feat = lax.broadcasted_iota(jnp.int32, qT.shape, 0)
    zero = jnp.zeros_like(qT)
    qbuf[0] = jnp.where(feat < DA_HEAD_QK, qT, zero)
    qbuf[1] = jnp.where(feat >= DA_HEAD_QK, qT, zero)
    acc1[...] = jnp.zeros(acc1.shape, F32)
    acc2[...] = jnp.zeros(acc2.shape, F32)
    comps = ((0, s1buf, acc1), (1, s2buf, acc2))

    def scores(j, c, sbuf, masked):
        sT = jnp.dot(k_ref[j], qbuf[c], preferred_element_type=F32)
        if masked:
            key = lax.broadcasted_iota(jnp.int32, sT.shape, 0)
            qry = lax.broadcasted_iota(jnp.int32, sT.shape, 1)
            sT = jnp.where(key <= qry, sT, -jnp.inf)
        sbuf[...] = sT
        return jnp.max(sT, axis=0, keepdims=True)

    def step(j, carry, nxt_masked):
        out = []
        for (c, sbuf, acc), (m_old, l_old, mx) in zip(comps, carry):
            m_new = jnp.maximum(m_old, mx)
            alpha = jnp.exp2(m_old - m_new)
            p = jnp.exp2(sbuf[...] - m_new)
            l_new = alpha * l_old + jnp.sum(p, axis=0, keepdims=True)
            p = p.astype(BF16)
            if nxt_masked is not None:
                mx = scores(j + 1, c, sbuf, nxt_masked)
            acc[...] = alpha * acc[...] + jnp.dot(vT_ref[j], p, preferred_element_type=F32)
            out.append((m_new, l_new, mx))
        return tuple(out)

    def first(masked):
        return tuple(scores(0, c, sbuf, masked) for c, sbuf, _ in comps)

    mx1, mx2 = lax.cond(i == 0, lambda: first(True), lambda: first(False))
    m0 = jnp.full((1, TQ), -jnp.inf, F32)
    l0 = jnp.zeros((1, TQ), F32)
    carry = ((m0, l0, mx1), (m0, l0, mx2))
    carry = lax.fori_loop(0, i - 1, lambda j, cr: step(j, cr, False), carry)
    carry = lax.cond(i > 0, lambda cr: step(i - 1, cr, True), lambda cr: cr, carry)
    (_, l1, _), (_, l2, _) = step(i, carry, None)

    lamv = lam_ref[...]
    lam = (jnp.exp(jnp.sum(lamv[0:1] * lamv[1:2], axis=-1, keepdims=True))
           - jnp.exp(jnp.sum(lamv[2:3] * lamv[3:4], axis=-1, keepdims=True))
           + lambda_init)
    o = acc1[...] / l1 - lam * (acc2[...] / l2)
    ms = jnp.mean(o * o, axis=0, keepdims=True)
    o = o * lax.rsqrt(ms + SUBLN_EPS) * subln_ref[...] * (1.0 - lambda_init)
    o_ref[...] = o.T.astype(o_ref.dtype)


def _attn(qT, k, vT, lamv, subln_col, lambda_init):
    B, NT, _, TQ = qT.shape
    S = NT * TQ
    k = k.reshape(B, NT, TQ, ATTN_WIDTH)
    accs = pltpu.VMEM((DA_HEAD_V, TQ), F32)
    return pl.pallas_call(
        functools.partial(_attn_kernel, lambda_init=lambda_init),
        grid=(B, DA_HEADS, NT),
        in_specs=[_const_spec((4, DA_HEAD_QK)), _const_spec((DA_HEAD_V, 1)),
                  pl.BlockSpec((None, None, LANES, TQ), lambda b, h, i: (b, i, h, 0)),
                  pl.BlockSpec((None, NT, TQ, LANES), lambda b, h, i: (b, 0, 0, h)),
                  pl.BlockSpec((None, NT, LANES, TQ), lambda b, h, i: (b, 0, h, 0))],
        out_specs=pl.BlockSpec((None, TQ, LANES), lambda b, h, i: (b, i, h)),
        out_shape=jax.ShapeDtypeStruct((B, S, ATTN_WIDTH), BF16),
        scratch_shapes=[accs, accs, pltpu.VMEM((TQ, TQ), F32), pltpu.VMEM((TQ, TQ), F32),
                        pltpu.VMEM((2, LANES, TQ), BF16)],
        compiler_params=pltpu.CompilerParams(
            dimension_semantics=("arbitrary", "arbitrary", "arbitrary"),
            vmem_limit_bytes=V7X_VMEM_LIMIT_BYTES),
        name="attn",
    )(lamv, subln_col, qT, k, vT)


def _mid0_kernel(x_ref, a_ref, b_ref, wo_ref, g_ref, wup_ref, wdn_ref, o_ref):
    mix = (jnp.dot(a_ref[...], wo_ref[0:POOL_WIDTH, :], preferred_element_type=F32)
           + jnp.dot(b_ref[...], wo_ref[POOL_WIDTH:, :], preferred_element_type=F32))
    h = x_ref[...] + mix
    o_ref[...] = h + _mlp(h, g_ref[...], wup_ref, wdn_ref)


def _mid0(x, a, b, w_out, g, w_up, w_down):
    N, D = x.shape
    T = TOK_TILE
    tok = lambda w: pl.BlockSpec((T, w), lambda t: (t, 0))
    return pl.pallas_call(
        _mid0_kernel,
        grid=(N // T,),
        in_specs=[tok(D), tok(POOL_WIDTH), tok(ATTN_WIDTH), _const_spec((D, D)),
                  _const_spec((1, D)), _const_spec((D, D_FF)), _const_spec((D_FF, D))],
        out_specs=tok(D),
        out_shape=jax.ShapeDtypeStruct((N, D), F32),
        compiler_params=pltpu.CompilerParams(
            dimension_semantics=("arbitrary",),
            vmem_limit_bytes=V7X_VMEM_LIMIT_BYTES),
        name="mid0",
    )(x, a, b, w_out, g, w_up, w_down)


def _layer1_kernel(x_ref, gmix_ref, pw1w_ref, pw1b_ref, dww_ref, dwb_ref, lng_ref, lnb_ref,
                   pw2w_ref, pw2b_ref, gmlp_ref, wup_ref, wdn_ref, gfin_ref, o_ref,
                   gbuf, ybuf):
    t = pl.program_id(1)
    T = x_ref.shape[0]
    x = x_ref[...]
    hn = _rms(x, gmix_ref[...], RMS_EPS).astype(BF16)
    a = jnp.dot(hn, pw1w_ref[...], preferred_element_type=F32) + pw1b_ref[...]
    glu = a[:, :CONV_WIDTH] * jax.nn.sigmoid(a[:, CONV_WIDTH:])

    @pl.when(t == 0)
    def _():
        gbuf[0:CONV_HALO, :] = jnp.zeros((CONV_HALO, CONV_WIDTH), F32)

    gbuf[CONV_HALO:CONV_HALO + T, :] = glu

    lead = CONV_HALO - (CONV_KERNEL - 1)

    def conv_chunk(r, carry):
        base = pl.multiple_of(r * CONV_ROWS, CONV_ROWS)
        acc = jnp.zeros((CONV_ROWS, CONV_WIDTH), F32)
        win = gbuf[pl.ds(base, CONV_ROWS + CONV_HALO), :]
        for k in range(CONV_KERNEL):
            acc = acc + dww_ref[k:k + 1, :] * win[lead + k:lead + k + CONV_ROWS, :]
        ybuf[pl.ds(base, CONV_ROWS), :] = acc
        return carry

    lax.fori_loop(0, T // CONV_ROWS, conv_chunk, 0)
    gbuf[0:CONV_HALO, :] = gbuf[T:T + CONV_HALO, :]

    y = ybuf[...] + dwb_ref[...]
    mu = jnp.mean(y, axis=-1, keepdims=True)
    yc = y - mu
    var = jnp.mean(yc * yc, axis=-1, keepdims=True)
    y = yc * lax.rsqrt(var + LN_EPS) * lng_ref[...] + lnb_ref[...]
    y = y * jax.nn.sigmoid(y)
    c = jnp.dot(y.astype(BF16), pw2w_ref[...], preferred_element_type=F32) + pw2b_ref[...]
    h = x + c
    h = h + _mlp(h, gmlp_ref[...], wup_ref, wdn_ref)
    o_ref[...] = _rms(h, gfin_ref[...], RMS_EPS)


def _layer1(x, gmix, pw1w, pw1b, dww, dwb, lng, lnb, pw2w, pw2b, gmlp, w_up, w_down, gfin):
    B, S, D = x.shape
    T = TOK_TILE
    tok = pl.BlockSpec((None, T, D), lambda b, t: (b, t, 0))
    vec = lambda w: _const_spec((1, w))
    return pl.pallas_call(
        _layer1_kernel,
        grid=(B, S // T),
        in_specs=[tok, vec(D), _const_spec((D, 2 * CONV_WIDTH)), vec(2 * CONV_WIDTH),
                  _const_spec((CONV_KERNEL, CONV_WIDTH)), vec(CONV_WIDTH), vec(CONV_WIDTH),
                  vec(CONV_WIDTH), _const_spec((CONV_WIDTH, D)), vec(D), vec(D),
                  _const_spec((D, D_FF)), _const_spec((D_FF, D)), vec(D)],
        out_specs=tok,
        out_shape=jax.ShapeDtypeStruct((B, S, D), F32),
        scratch_shapes=[pltpu.VMEM((CONV_HALO + T, CONV_WIDTH), F32),
                        pltpu.VMEM((T, CONV_WIDTH), F32)],
        compiler_params=pltpu.CompilerParams(
            dimension_semantics=("arbitrary", "arbitrary"),
            vmem_limit_bytes=V7X_VMEM_LIMIT_BYTES),
        name="layer1",
    )(x, gmix, pw1w, pw1b, dww, dwb, lng, lnb, pw2w, pw2b, gmlp, w_up, w_down, gfin)


def _rope_lane_tables(S):
    half = ROT_DIM // 2
    pos = jnp.arange(S, dtype=F32)
    inv_freq = ROPE_THETA ** (-jnp.arange(0, ROT_DIM, 2, dtype=F32) / ROT_DIM)
    ang = pos[:, None] * inv_freq[None, :]
    cos, sin = jnp.cos(ang), jnp.sin(ang)
    rest = DA_HEAD_QK - ROT_DIM
    one = jnp.ones((S, rest), F32)
    zero = jnp.zeros((S, rest), F32)
    zh = jnp.zeros((S, half), F32)
    rc = jnp.concatenate([cos, cos, one], axis=-1)
    rs1 = jnp.concatenate([-sin, zh, zero], axis=-1)
    rs2 = jnp.concatenate([zh, sin, zero], axis=-1)
    rep = LANES // DA_HEAD_QK
    return tuple(jnp.tile(tbl, (1, rep)) for tbl in (rc, rs1, rs2))


def kernel(x, mix_norm, mlp_norm, w_up, w_down, final_norm, w_in, pool_w, pool_scale,
           lam_q1, lam_k1, lam_q2, lam_k2, subln, w_out, conv_pw1_w, conv_pw1_b,
           conv_dw_w, conv_dw_b, conv_ln_g, conv_ln_b, conv_pw2_w, conv_pw2_b):
    B, S, D = x.shape
    row = lambda v: v.reshape(1, -1)
    rc, rs1, rs2 = _rope_lane_tables(S)

    a, qT, k, vT = _front0(x, row(mix_norm[0]), w_in[0].astype(BF16), pool_w[0].astype(BF16),
                         row(pool_scale[0]), rc, rs1, rs2)
    lamv = jnp.stack([lam_q1[0], lam_k1[0], lam_q2[0], lam_k2[0]]).astype(F32)
    lambda_init = 0.8 - 0.6 * math.exp(-0.3 * 0)
    b = _attn(qT, k, vT, lamv, subln[0].reshape(-1, 1), lambda_init)
    h = _mid0(x.reshape(B * S, D), a.reshape(B * S, POOL_WIDTH), b.reshape(B * S, ATTN_WIDTH),
              w_out[0].astype(BF16), row(mlp_norm[0]), w_up[0].astype(BF16),
              w_down[0].astype(BF16))

    return _layer1(h.reshape(B, S, D), row(mix_norm[1]), conv_pw1_w[0].astype(BF16),
                   row(conv_pw1_b[0]), conv_dw_w[0], row(conv_dw_b[0]), row(conv_ln_g[0]),
                   row(conv_ln_b[0]), conv_pw2_w[0].astype(BF16), row(conv_pw2_b[0]),
                   row(mlp_norm[1]), w_up[1].astype(BF16), w_down[1].astype(BF16),
                   row(final_norm))
```

```python
import functools
import math

import jax
import jax.numpy as jnp
from jax import lax
from jax.experimental import pallas as pl
from jax.experimental.pallas import tpu as pltpu

F32 = jnp.float32
BF16 = jnp.bfloat16

D_MODEL = 1024
POOL_WIDTH = 512
POOL_WINDOWS = (2, 4, 8, 16)
POOL_GROUP = 128
ATTN_WIDTH = 512
DA_HEAD_V = 128
DA_HEADS = 4
DA_HEAD_QK = 64
ROT_DIM = 16
ROPE_THETA = 500000.0
IN_WIDTH = 2048
CONV_WIDTH = 1024
CONV_KERNEL = 31
D_FF = 4096
RMS_EPS = 1e-6
LN_EPS = 1e-5
SUBLN_EPS = 1e-5

V7X_VMEM_LIMIT_BYTES = 56 * 1024 * 1024
SUBLANES = 8
LANES = 128

TOK_TILE = 512
ATT_TILE = 512
FF_CHUNK = 1024
FF_LOOP_CHUNK = 512
POOL_HALO = 16
CONV_HALO = 32
CONV_ROWS = 32


def _rms(x, g, eps):
    ms = jnp.mean(x * x, axis=-1, keepdims=True)
    return x * lax.rsqrt(ms + eps) * g


def _const_spec(shape):
    nd = len(shape)
    return pl.BlockSpec(shape, lambda *_: (0,) * nd, pipeline_mode=pl.Buffered(1))


def _mlp(h, g, wup_ref, wdn_ref):
    hn = _rms(h, g, RMS_EPS).astype(BF16)
    acc = None
    for c in range(D_FF // FF_CHUNK):
        cols = slice(c * FF_CHUNK, (c + 1) * FF_CHUNK)
        up = jnp.dot(hn, wup_ref[:, cols], preferred_element_type=F32)
        r = jnp.maximum(up, 0.0)
        d = jnp.dot((r * r).astype(BF16), wdn_ref[cols, :], preferred_element_type=F32)
        acc = d if acc is None else acc + d
    return acc


def _front0_kernel(x_ref, g_ref, win_ref, pw_ref, ps_ref, rc_ref, rs1_ref, rs2_ref,
                   a_ref, qT_ref, k_ref, vT_ref, ubuf):
    t = pl.program_id(1)
    T = x_ref.shape[0]
    hn = _rms(x_ref[...], g_ref[...], RMS_EPS).astype(BF16)
    z = jnp.dot(hn, win_ref[...], preferred_element_type=F32)

    @pl.when(t == 0)
    def _():
        ubuf[0:POOL_HALO, :] = jnp.zeros((POOL_HALO, POOL_WIDTH), F32)

    ubuf[POOL_HALO:POOL_HALO + T, :] = z[:, :POOL_WIDTH]
    pos = t * T + lax.broadcasted_iota(jnp.int32, (T, 1), 0)
    for g, w in enumerate(POOL_WINDOWS):
        lanes = slice(g * POOL_GROUP, (g + 1) * POOL_GROUP)
        ug = ubuf[POOL_HALO:POOL_HALO + T, lanes]
        acc = ug
        for j in range(1, w):
            acc = acc + ubuf[POOL_HALO - j:POOL_HALO - j + T, lanes]
        cnt = jnp.minimum(pos + 1, w).astype(F32)
        pooled = acc / cnt - ug
        ag = jnp.dot(pooled.astype(BF16), pw_ref[g], preferred_element_type=F32)
        a_ref[:, lanes] = (ag * ps_ref[:, lanes]).astype(BF16)
    ubuf[0:POOL_HALO, :] = ubuf[T:T + POOL_HALO, :]

    rc = rc_ref[...]
    rs1 = rs1_ref[...]
    rs2 = rs2_ref[...]
    scale = DA_HEAD_QK ** -0.5 * math.log2(math.e)
    def rope(xh):
        return (xh * rc + pltpu.roll(xh, LANES - ROT_DIM // 2, 1) * rs1
                + pltpu.roll(xh, ROT_DIM // 2, 1) * rs2)

    for h in range(DA_HEADS):
        lanes = slice(h * LANES, (h + 1) * LANES)
        qh = rope(z[:, POOL_WIDTH + h * LANES: POOL_WIDTH + (h + 1) * LANES]) * scale
        qT_ref[lanes, :] = qh.T.astype(BF16)
        off = POOL_WIDTH + ATTN_WIDTH
        k_ref[:, lanes] = rope(z[:, off + h * LANES: off + (h + 1) * LANES]).astype(BF16)
        off = POOL_WIDTH + 2 * ATTN_WIDTH
        vT_ref[lanes, :] = z[:, off + h * LANES: off + (h + 1) * LANES].T.astype(BF16)


def _front0(x, g, w_in, pool_w, pool_scale, rc, rs1, rs2):
    B, S, D = x.shape
    T = TOK_TILE
    out = jax.ShapeDtypeStruct((B, S, ATTN_WIDTH), BF16)
    outT = jax.ShapeDtypeStruct((B, S // T, ATTN_WIDTH, T), BF16)
    tok = lambda w: pl.BlockSpec((None, T, w), lambda b, t: (b, t, 0))
    tokT = pl.BlockSpec((None, None, ATTN_WIDTH, T), lambda b, t: (b, t, 0, 0))
    rope = pl.BlockSpec((T, LANES), lambda b, t: (t, 0))
    return pl.pallas_call(
        _front0_kernel,
        grid=(B, S // T),
        in_specs=[tok(D), _const_spec((1, D)), _const_spec((D, IN_WIDTH)),
                  _const_spec((len(POOL_WINDOWS), POOL_GROUP, POOL_GROUP)),
                  _const_spec((1, POOL_WIDTH)), rope, rope, rope],
        out_specs=[tok(POOL_WIDTH), tokT, tok(ATTN_WIDTH), tokT],
        out_shape=[out, outT, out, outT],
        scratch_shapes=[pltpu.VMEM((POOL_HALO + T, POOL_WIDTH), F32)],
        compiler_params=pltpu.CompilerParams(
            dimension_semantics=("arbitrary", "arbitrary"),
            vmem_limit_bytes=V7X_VMEM_LIMIT_BYTES),
        name="front0",
    )(x, g, w_in, pool_w, pool_scale, rc, rs1, rs2)


def _attn_kernel(lam_ref, subln_ref, qT_ref, k_ref, vT_ref, o_ref, acc1, acc2, s1buf, s2buf,
                 qbuf, *, lambda_init):
    i = pl.program_id(2)
    TQ = qT_ref.shape[1]
    qT = qT_ref[...]
    feat = lax.broadcasted_iota(jnp.int32, qT.shape, 0)
    zero = jnp.zeros_like(qT)
    qbuf[0] = jnp.where(feat < DA_HEAD_QK, qT, zero)
    qbuf[1] = jnp.where(feat >= DA_HEAD_QK, qT, zero)
    acc1[...] = jnp.zeros(acc1.shape, F32)
    acc2[...] = jnp.zeros(acc2.shape, F32)
    comps = ((0, s1buf, acc1), (1, s2buf, acc2))

    def scores(j, c, sbuf, masked):
        sT = jnp.dot(k_ref[j], qbuf[c], preferred_element_type=F32)
        if masked:
            key = lax.broadcasted_iota(jnp.int32, sT.shape, 0)
            qry = lax.broadcasted_iota(jnp.int32, sT.shape, 1)
            sT = jnp.where(key <= qry, sT, -jnp.inf)
        sbuf[...] = sT
        return jnp.max(sT, axis=0, keepdims=True)

    def step(j, carry, nxt_masked):
        out = []
        for (c, sbuf, acc), (m_old, l_old, mx) in zip(comps, carry):
            m_new = jnp.maximum(m_old, mx)
            alpha = jnp.exp2(m_old - m_new)
            p = jnp.exp2(sbuf[...] - m_new)
            l_new = alpha * l_old + jnp.sum(p, axis=0, keepdims=True)
            p = p.astype(BF16)
            if nxt_masked is not None:
                mx = scores(j + 1, c, sbuf, nxt_masked)
            acc[...] = alpha * acc[...] + jnp.dot(vT_ref[j], p, preferred_element_type=F32)
            out.append((m_new, l_new, mx))
        return tuple(out)

    def first(masked):
        return tuple(scores(0, c, sbuf, masked) for c, sbuf, _ in comps)

    mx1, mx2 = lax.cond(i == 0, lambda: first(True), lambda: first(False))
    m0 = jnp.full((1, TQ), -jnp.inf, F32)
    l0 = jnp.zeros((1, TQ), F32)
    carry = ((m0, l0, mx1), (m0, l0, mx2))
    carry = lax.fori_loop(0, i - 1, lambda j, cr: step(j, cr, False), carry)
    carry = lax.cond(i > 0, lambda cr: step(i - 1, cr, True), lambda cr: cr, carry)
    (_, l1, _), (_, l2, _) = step(i, carry, None)

    lamv = lam_ref[...]
    lam = (jnp.exp(jnp.sum(lamv[0:1] * lamv[1:2], axis=-1, keepdims=True))
           - jnp.exp(jnp.sum(lamv[2:3] * lamv[3:4], axis=-1, keepdims=True))
           + lambda_init)
    o = acc1[...] / l1 - lam * (acc2[...] / l2)
    ms = jnp.mean(o * o, axis=0, keepdims=True)
    o = o * lax.rsqrt(ms + SUBLN_EPS) * subln_ref[...] * (1.0 - lambda_init)
    o_ref[...] = o.T.astype(o_ref.dtype)


def _attn(qT, k, vT, lamv, subln_col, lambda_init):
    B, NT, _, TQ = qT.shape
    S = NT * TQ
    k = k.reshape(B, NT, TQ, ATTN_WIDTH)
    accs = pltpu.VMEM((DA_HEAD_V, TQ), F32)
    return pl.pallas_call(
        functools.partial(_attn_kernel, lambda_init=lambda_init),
        grid=(B, DA_HEADS, NT),
        in_specs=[_const_spec((4, DA_HEAD_QK)), _const_spec((DA_HEAD_V, 1)),
                  pl.BlockSpec((None, None, LANES, TQ), lambda b, h, i: (b, i, h, 0)),
                  pl.BlockSpec((None, NT, TQ, LANES), lambda b, h, i: (b, 0, 0, h)),
                  pl.BlockSpec((None, NT, LANES, TQ), lambda b, h, i: (b, 0, h, 0))],
        out_specs=pl.BlockSpec((None, TQ, LANES), lambda b, h, i: (b, i, h)),
        out_shape=jax.ShapeDtypeStruct((B, S, ATTN_WIDTH), BF16),
        scratch_shapes=[accs, accs, pltpu.VMEM((TQ, TQ), F32), pltpu.VMEM((TQ, TQ), F32),
                        pltpu.VMEM((2, LANES, TQ), BF16)],
        compiler_params=pltpu.CompilerParams(
            dimension_semantics=("arbitrary", "arbitrary", "arbitrary"),
            vmem_limit_bytes=V7X_VMEM_LIMIT_BYTES),
        name="attn",
    )(lamv, subln_col, qT, k, vT)


def _mid0_kernel(x_ref, a_ref, b_ref, wo_ref, g_ref, wup_ref, wdn_ref, o_ref):
    mix = (jnp.dot(a_ref[...], wo_ref[0:POOL_WIDTH, :], preferred_element_type=F32)
           + jnp.dot(b_ref[...], wo_ref[POOL_WIDTH:, :], preferred_element_type=F32))
    h = x_ref[...] + mix
    o_ref[...] = h + _mlp(h, g_ref[...], wup_ref, wdn_ref)


def _mid0(x, a, b, w_out, g, w_up, w_down):
    N, D = x.shape
    T = TOK_TILE
    tok = lambda w: pl.BlockSpec((T, w), lambda t: (t, 0))
    return pl.pallas_call(
        _mid0_kernel,
        grid=(N // T,),
        in_specs=[tok(D), tok(POOL_WIDTH), tok(ATTN_WIDTH), _const_spec((D, D)),
                  _const_spec((1, D)), _const_spec((D, D_FF)), _const_spec((D_FF, D))],
        out_specs=tok(D),
        out_shape=jax.ShapeDtypeStruct((N, D), F32),
        compiler_params=pltpu.CompilerParams(
            dimension_semantics=("arbitrary",),
            vmem_limit_bytes=V7X_VMEM_LIMIT_BYTES),
        name="mid0",
    )(x, a, b, w_out, g, w_up, w_down)


def _conv_rows(gbuf, dww_ref, ybuf, base):
    lead = CONV_HALO - (CONV_KERNEL - 1)
    nwin = CONV_ROWS + CONV_HALO
    for lt in range(CONV_WIDTH // LANES):
        lanes = slice(lt * LANES, (lt + 1) * LANES)
        win = gbuf[pl.ds(base, nwin), lanes]
        acc = None
        for r in range(SUBLANES):
            shifted = win if r == 0 else pltpu.roll(win, nwin - r, 0)
            for a in range(CONV_HALO // SUBLANES + 1):
                k = SUBLANES * a + r - lead
                if 0 <= k < CONV_KERNEL:
                    term = dww_ref[k:k + 1, lanes] * shifted[SUBLANES * a:SUBLANES * a + CONV_ROWS, :]
                    acc = term if acc is None else acc + term
        ybuf[pl.ds(base, CONV_ROWS), lanes] = acc


def _layer1_kernel(x_ref, gmix_ref, pw1w_ref, pw1b_ref, dww_ref, dwb_ref, lng_ref, lnb_ref,
                   pw2w_ref, pw2b_ref, gmlp_ref, wup_ref, wdn_ref, gfin_ref, o_ref,
                   gbuf, ybuf, hbuf, hpbuf, accbuf):
    b = pl.program_id(0)
    t = pl.program_id(1)
    T = x_ref.shape[0]

    @pl.when(jnp.logical_and(b == 0, t == 0))
    def _():
        hbuf[...] = jnp.zeros(hbuf.shape, F32)

    x = x_ref[...]
    hn = _rms(x, gmix_ref[...], RMS_EPS).astype(BF16)
    a = jnp.dot(hn, pw1w_ref[...], preferred_element_type=F32) + pw1b_ref[...]
    glu = a[:, :CONV_WIDTH] * jax.nn.sigmoid(a[:, CONV_WIDTH:])

    @pl.when(t == 0)
    def _():
        gbuf[0:CONV_HALO, :] = jnp.zeros((CONV_HALO, CONV_WIDTH), F32)

    gbuf[CONV_HALO:CONV_HALO + T, :] = glu

    hpbuf[...] = _rms(hbuf[...], gmlp_ref[...], RMS_EPS).astype(BF16)
    accbuf[...] = jnp.zeros(accbuf.shape, F32)
    n_ff = wup_ref.shape[0]
    rows_per_ff = T // n_ff

    def ff_step(c, carry):
        up = jnp.dot(hpbuf[...], wup_ref[c], preferred_element_type=F32)
        r = jnp.maximum(up, 0.0)
        accbuf[...] += jnp.dot((r * r).astype(BF16), wdn_ref[c], preferred_element_type=F32)
        base = pl.multiple_of(c * rows_per_ff, rows_per_ff)
        for sub in range(rows_per_ff // CONV_ROWS):
            _conv_rows(gbuf, dww_ref, ybuf, base + sub * CONV_ROWS)
        return carry

    lax.fori_loop(0, n_ff, ff_step, 0)
    o_ref[...] = _rms(hbuf[...] + accbuf[...], gfin_ref[...], RMS_EPS)

    gbuf[0:CONV_HALO, :] = gbuf[T:T + CONV_HALO, :]
    y = ybuf[...] + dwb_ref[...]
    mu = jnp.mean(y, axis=-1, keepdims=True)
    yc = y - mu
    var = jnp.mean(yc * yc, axis=-1, keepdims=True)
    y = yc * lax.rsqrt(var + LN_EPS) * lng_ref[...] + lnb_ref[...]
    y = y * jax.nn.sigmoid(y)
    c = jnp.dot(y.astype(BF16), pw2w_ref[...], preferred_element_type=F32) + pw2b_ref[...]
    hbuf[...] = x + c


def _layer1(x, gmix, pw1w, pw1b, dww, dwb, lng, lnb, pw2w, pw2b, gmlp, w_up, w_down, gfin):
    B, S, D = x.shape
    T = TOK_TILE
    NT = S // T
    n_ff = D_FF // FF_LOOP_CHUNK
    assert T % (n_ff * CONV_ROWS) == 0
    w_up = w_up.reshape(D, n_ff, FF_LOOP_CHUNK).transpose(1, 0, 2)
    w_down = w_down.reshape(n_ff, FF_LOOP_CHUNK, D)
    tok_in = pl.BlockSpec((None, T, D), lambda b, t: (b, jnp.minimum(t, NT - 1), 0))
    tok_out = pl.BlockSpec((None, T, D), lambda b, t: (b, jnp.maximum(t - 1, 0), 0))
    vec = lambda w: _const_spec((1, w))
    return pl.pallas_call(
        _layer1_kernel,
        grid=(B, NT + 1),
        in_specs=[tok_in, vec(D), _const_spec((D, 2 * CONV_WIDTH)), vec(2 * CONV_WIDTH),
                  _const_spec((CONV_KERNEL, CONV_WIDTH)), vec(CONV_WIDTH), vec(CONV_WIDTH),
                  vec(CONV_WIDTH), _const_spec((CONV_WIDTH, D)), vec(D), vec(D),
                  _const_spec((n_ff, D, FF_LOOP_CHUNK)), _const_spec((n_ff, FF_LOOP_CHUNK, D)),
                  vec(D)],
        out_specs=tok_out,
        out_shape=jax.ShapeDtypeStruct((B, S, D), F32),
        scratch_shapes=[pltpu.VMEM((CONV_HALO + T, CONV_WIDTH), F32),
                        pltpu.VMEM((T, CONV_WIDTH), F32),
                        pltpu.VMEM((T, D), F32),
                        pltpu.VMEM((T, D), BF16),
                        pltpu.VMEM((T, D), F32)],
        compiler_params=pltpu.CompilerParams(
            dimension_semantics=("arbitrary", "arbitrary"),
            vmem_limit_bytes=V7X_VMEM_LIMIT_BYTES),
        name="layer1",
    )(x, gmix, pw1w, pw1b, dww, dwb, lng, lnb, pw2w, pw2b, gmlp, w_up, w_down, gfin)


def _rope_lane_tables(S):
    half = ROT_DIM // 2
    pos = jnp.arange(S, dtype=F32)
    inv_freq = ROPE_THETA ** (-jnp.arange(0, ROT_DIM, 2, dtype=F32) / ROT_DIM)
    ang = pos[:, None] * inv_freq[None, :]
    cos, sin = jnp.cos(ang), jnp.sin(ang)
    rest = DA_HEAD_QK - ROT_DIM
    one = jnp.ones((S, rest), F32)
    zero = jnp.zeros((S, rest), F32)
    zh = jnp.zeros((S, half), F32)
    rc = jnp.concatenate([cos, cos, one], axis=-1)
    rs1 = jnp.concatenate([-sin, zh, zero], axis=-1)
    rs2 = jnp.concatenate([zh, sin, zero], axis=-1)
    rep = LANES // DA_HEAD_QK
    return tuple(jnp.tile(tbl, (1, rep)) for tbl in (rc, rs1, rs2))


def kernel(x, mix_norm, mlp_norm, w_up, w_down, final_norm, w_in, pool_w, pool_scale,
           lam_q1, lam_k1, lam_q2, lam_k2, subln, w_out, conv_pw1_w, conv_pw1_b,
           conv_dw_w, conv_dw_b, conv_ln_g, conv_ln_b, conv_pw2_w, conv_pw2_b):
    B, S, D = x.shape
    row = lambda v: v.reshape(1, -1)
    rc, rs1, rs2 = _rope_lane_tables(S)

    a, qT, k, vT = _front0(x, row(mix_norm[0]), w_in[0].astype(BF16), pool_w[0].astype(BF16),
                         row(pool_scale[0]), rc, rs1, rs2)
    lamv = jnp.stack([lam_q1[0], lam_k1[0], lam_q2[0], lam_k2[0]]).astype(F32)
    lambda_init = 0.8 - 0.6 * math.exp(-0.3 * 0)
    b = _attn(qT, k, vT, lamv, subln[0].reshape(-1, 1), lambda_init)
    h = _mid0(x.reshape(B * S, D), a.reshape(B * S, POOL_WIDTH), b.reshape(B * S, ATTN_WIDTH),
              w_out[0].astype(BF16), row(mlp_norm[0]), w_up[0].astype(BF16),
              w_down[0].astype(BF16))

    return _layer1(h.reshape(B, S, D), row(mix_norm[1]), conv_pw1_w[0].astype(BF16),
                   row(conv_pw1_b[0]), conv_dw_w[0], row(conv_dw_b[0]), row(conv_ln_g[0]),
                   row(conv_ln_b[0]), conv_pw2_w[0].astype(BF16), row(conv_pw2_b[0]),
                   row(mlp_norm[1]), w_up[1].astype(BF16), w_down[1].astype(BF16),
                   row(final_norm))
```

```python
import functools
import math

import jax
import jax.numpy as jnp
from jax import lax
from jax.experimental import pallas as pl
from jax.experimental.pallas import tpu as pltpu

F32 = jnp.float32
BF16 = jnp.bfloat16

D_MODEL = 1024
POOL_WIDTH = 512
POOL_WINDOWS = (2, 4, 8, 16)
POOL_GROUP = 128
ATTN_WIDTH = 512
DA_HEAD_V = 128
DA_HEADS = 4
DA_HEAD_QK = 64
ROT_DIM = 16
ROPE_THETA = 500000.0
IN_WIDTH = 2048
CONV_WIDTH = 1024
CONV_KERNEL = 31
D_FF = 4096
RMS_EPS = 1e-6
LN_EPS = 1e-5
SUBLN_EPS = 1e-5

V7X_VMEM_LIMIT_BYTES = 56 * 1024 * 1024
SUBLANES = 8
LANES = 128

TOK_TILE = 512
ATT_TILE = 512
FF_CHUNK = 1024
FF_LOOP_CHUNK = 512
FRONT_ROW_BLOCKS = 2
LAYER1_ROW_BLOCKS = 2
POOL_HALO = 16
CONV_HALO = 32
CONV_ROWS = 32


def _rms(x, g, eps):
    ms = jnp.mean(x * x, axis=-1, keepdims=True)
    return x * lax.rsqrt(ms + eps) * g


def _const_spec(shape):
    nd = len(shape)
    return pl.BlockSpec(shape, lambda *_: (0,) * nd, pipeline_mode=pl.Buffered(1))


def _mlp(h, g, wup_ref, wdn_ref):
    hn = _rms(h, g, RMS_EPS).astype(BF16)
    acc = None
    for c in range(D_FF // FF_CHUNK):
        cols = slice(c * FF_CHUNK, (c + 1) * FF_CHUNK)
        up = jnp.dot(hn, wup_ref[:, cols], preferred_element_type=F32)
        r = jnp.maximum(up, 0.0)
        d = jnp.dot((r * r).astype(BF16), wdn_ref[cols, :], preferred_element_type=F32)
        acc = d if acc is None else acc + d
    return acc


def _front0_kernel(x_ref, g_ref, win_ref, pw_ref, ps_ref, rc_ref, rs1_ref, rs2_ref,
                   a_ref, qT_ref, k_ref, vT_ref, ubuf):
    t = pl.program_id(1)
    T = x_ref.shape[0]
    R = T // FRONT_ROW_BLOCKS
    scale = DA_HEAD_QK ** -0.5 * math.log2(math.e)

    @pl.when(t == 0)
    def _():
        ubuf[0:POOL_HALO, :] = jnp.zeros((POOL_HALO, POOL_WIDTH), F32)

    zs = []
    for rb in range(FRONT_ROW_BLOCKS):
        hn = _rms(x_ref[rb * R:(rb + 1) * R, :], g_ref[...], RMS_EPS).astype(BF16)
        zs.append(jnp.dot(hn, win_ref[...], preferred_element_type=F32))

    for rb, z in enumerate(zs):
        rows = slice(rb * R, (rb + 1) * R)
        u0 = POOL_HALO + rb * R
        ubuf[u0:u0 + R, :] = z[:, :POOL_WIDTH]
        pos = t * T + rb * R + lax.broadcasted_iota(jnp.int32, (R, 1), 0)
        for g, w in enumerate(POOL_WINDOWS):
            lanes = slice(g * POOL_GROUP, (g + 1) * POOL_GROUP)
            ug = ubuf[u0:u0 + R, lanes]
            acc = ug
            for j in range(1, w):
                acc = acc + ubuf[u0 - j:u0 - j + R, lanes]
            cnt = jnp.minimum(pos + 1, w).astype(F32)
            pooled = acc / cnt - ug
            ag = jnp.dot(pooled.astype(BF16), pw_ref[g], preferred_element_type=F32)
            a_ref[rows, lanes] = (ag * ps_ref[:, lanes]).astype(BF16)

        rc = rc_ref[rows, :]
        rs1 = rs1_ref[rows, :]
        rs2 = rs2_ref[rows, :]

        def rope(xh):
            return (xh * rc + pltpu.roll(xh, LANES - ROT_DIM // 2, 1) * rs1
                    + pltpu.roll(xh, ROT_DIM // 2, 1) * rs2)

        for h in range(DA_HEADS):
            lanes = slice(h * LANES, (h + 1) * LANES)
            qh = rope(z[:, POOL_WIDTH + h * LANES: POOL_WIDTH + (h + 1) * LANES]) * scale
            qT_ref[lanes, rows] = qh.T.astype(BF16)
            off = POOL_WIDTH + ATTN_WIDTH
            k_ref[rows, lanes] = rope(z[:, off + h * LANES: off + (h + 1) * LANES]).astype(BF16)
            off = POOL_WIDTH + 2 * ATTN_WIDTH
            vT_ref[lanes, rows] = z[:, off + h * LANES: off + (h + 1) * LANES].T.astype(BF16)
    ubuf[0:POOL_HALO, :] = ubuf[T:T + POOL_HALO, :]


def _front0(x, g, w_in, pool_w, pool_scale, rc, rs1, rs2):
    B, S, D = x.shape
    T = TOK_TILE
    out = jax.ShapeDtypeStruct((B, S, ATTN_WIDTH), BF16)
    outT = jax.ShapeDtypeStruct((B, S // T, ATTN_WIDTH, T), BF16)
    tok = lambda w: pl.BlockSpec((None, T, w), lambda b, t: (b, t, 0))
    tokT = pl.BlockSpec((None, None, ATTN_WIDTH, T), lambda b, t: (b, t, 0, 0))
    rope = pl.BlockSpec((T, LANES), lambda b, t: (t, 0))
    return pl.pallas_call(
        _front0_kernel,
        grid=(B, S // T),
        in_specs=[tok(D), _const_spec((1, D)), _const_spec((D, IN_WIDTH)),
                  _const_spec((len(POOL_WINDOWS), POOL_GROUP, POOL_GROUP)),
                  _const_spec((1, POOL_WIDTH)), rope, rope, rope],
        out_specs=[tok(POOL_WIDTH), tokT, tok(ATTN_WIDTH), tokT],
        out_shape=[out, outT, out, outT],
        scratch_shapes=[pltpu.VMEM((POOL_HALO + T, POOL_WIDTH), F32)],
        compiler_params=pltpu.CompilerParams(
            dimension_semantics=("arbitrary", "arbitrary"),
            vmem_limit_bytes=V7X_VMEM_LIMIT_BYTES),
        name="front0",
    )(x, g, w_in, pool_w, pool_scale, rc, rs1, rs2)


def _attn_kernel(lam_ref, subln_ref, qT_ref, qTn_ref, k_ref, vT_ref, o_ref, acc1, acc2,
                 s1buf, s2buf, qbuf, mxbuf, *, lambda_init):
    i = pl.program_id(2)
    TQ = qT_ref.shape[1]
    acc1[...] = jnp.zeros(acc1.shape, F32)
    acc2[...] = jnp.zeros(acc2.shape, F32)
    comps = ((0, s1buf, acc1), (1, s2buf, acc2))

    def set_q(src_ref):
        qT = src_ref[...]
        feat = lax.broadcasted_iota(jnp.int32, qT.shape, 0)
        zero = jnp.zeros_like(qT)
        qbuf[0] = jnp.where(feat < DA_HEAD_QK, qT, zero)
        qbuf[1] = jnp.where(feat >= DA_HEAD_QK, qT, zero)

    def scores(j, c, sbuf, masked):
        sT = jnp.dot(k_ref[j], qbuf[c], preferred_element_type=F32)
        if masked:
            key = lax.broadcasted_iota(jnp.int32, sT.shape, 0)
            qry = lax.broadcasted_iota(jnp.int32, sT.shape, 1)
            sT = jnp.where(key <= qry, sT, -jnp.inf)
        sbuf[...] = sT
        return jnp.max(sT, axis=0, keepdims=True)

    def step(j, carry, nxt):
        if nxt == "next_q":
            set_q(qTn_ref)
        out = []
        for (c, sbuf, acc), (m_old, l_old, mx) in zip(comps, carry):
            m_new = jnp.maximum(m_old, mx)
            alpha = jnp.exp2(m_old - m_new)
            p = jnp.exp2(sbuf[...] - m_new)
            l_new = alpha * l_old + jnp.sum(p, axis=0, keepdims=True)
            p = p.astype(BF16)
            if nxt == "next_q":
                mxbuf[c] = scores(0, c, sbuf, False)
            elif nxt is not None:
                mx = scores(j + 1, c, sbuf, nxt)
            acc[...] = alpha * acc[...] + jnp.dot(vT_ref[j], p, preferred_element_type=F32)
            out.append((m_new, l_new, mx))
        return tuple(out)

    @pl.when(i == 0)
    def _():
        set_q(qT_ref)
        for c, sbuf, _ in comps:
            mxbuf[c] = scores(0, c, sbuf, True)

    m0 = jnp.full((1, TQ), -jnp.inf, F32)
    l0 = jnp.zeros((1, TQ), F32)
    carry = ((m0, l0, mxbuf[0]), (m0, l0, mxbuf[1]))
    n_plain = jnp.maximum(i - 1, 0)
    odd = n_plain % 2
    carry = lax.cond(odd == 1, lambda cr: step(0, cr, False), lambda cr: cr, carry)
    carry = lax.fori_loop(
        0, n_plain // 2,
        lambda jj, cr: step(odd + 2 * jj + 1, step(odd + 2 * jj, cr, False), False), carry)
    carry = lax.cond(i > 0, lambda cr: step(i - 1, cr, True), lambda cr: cr, carry)
    (_, l1, _), (_, l2, _) = step(i, carry, "next_q")

    lamv = lam_ref[...]
    lam = (jnp.exp(jnp.sum(lamv[0:1] * lamv[1:2], axis=-1, keepdims=True))
           - jnp.exp(jnp.sum(lamv[2:3] * lamv[3:4], axis=-1, keepdims=True))
           + lambda_init)
    o = acc1[...] / l1 - lam * (acc2[...] / l2)
    ms = jnp.mean(o * o, axis=0, keepdims=True)
    o = o * lax.rsqrt(ms + SUBLN_EPS) * subln_ref[...] * (1.0 - lambda_init)
    o_ref[...] = o.T.astype(o_ref.dtype)


def _attn(qT, k, vT, lamv, subln_col, lambda_init):
    B, NT, _, TQ = qT.shape
    S = NT * TQ
    k = k.reshape(B, NT, TQ, ATTN_WIDTH)
    accs = pltpu.VMEM((DA_HEAD_V, TQ), F32)
    return pl.pallas_call(
        functools.partial(_attn_kernel, lambda_init=lambda_init),
        grid=(B, DA_HEADS, NT),
        in_specs=[_const_spec((4, DA_HEAD_QK)), _const_spec((DA_HEAD_V, 1)),
                  pl.BlockSpec((None, None, LANES, TQ), lambda b, h, i: (b, i, h, 0)),
                  pl.BlockSpec((None, None, LANES, TQ),
                               lambda b, h, i: (b, jnp.minimum(i + 1, NT - 1), h, 0)),
                  pl.BlockSpec((None, NT, TQ, LANES), lambda b, h, i: (b, 0, 0, h)),
                  pl.BlockSpec((None, NT, LANES, TQ), lambda b, h, i: (b, 0, h, 0))],
        out_specs=pl.BlockSpec((None, TQ, LANES), lambda b, h, i: (b, i, h)),
        out_shape=jax.ShapeDtypeStruct((B, S, ATTN_WIDTH), BF16),
        scratch_shapes=[accs, accs, pltpu.VMEM((TQ, TQ), F32), pltpu.VMEM((TQ, TQ), F32),
                        pltpu.VMEM((2, LANES, TQ), BF16), pltpu.VMEM((2, 1, TQ), F32)],
        compiler_params=pltpu.CompilerParams(
            dimension_semantics=("arbitrary", "arbitrary", "arbitrary"),
            vmem_limit_bytes=V7X_VMEM_LIMIT_BYTES),
        name="attn",
    )(lamv, subln_col, qT, qT, k, vT)


def _mid0_kernel(x_ref, a_ref, b_ref, wo_ref, g_ref, wup_ref, wdn_ref, o_ref):
    mix = (jnp.dot(a_ref[...], wo_ref[0:POOL_WIDTH, :], preferred_element_type=F32)
           + jnp.dot(b_ref[...], wo_ref[POOL_WIDTH:, :], preferred_element_type=F32))
    h = x_ref[...] + mix
    o_ref[...] = h + _mlp(h, g_ref[...], wup_ref, wdn_ref)


def _mid0(x, a, b, w_out, g, w_up, w_down):
    N, D = x.shape
    T = TOK_TILE
    tok = lambda w: pl.BlockSpec((T, w), lambda t: (t, 0))
    return pl.pallas_call(
        _mid0_kernel,
        grid=(N // T,),
        in_specs=[tok(D), tok(POOL_WIDTH), tok(ATTN_WIDTH), _const_spec((D, D)),
                  _const_spec((1, D)), _const_spec((D, D_FF)), _const_spec((D_FF, D))],
        out_specs=tok(D),
        out_shape=jax.ShapeDtypeStruct((N, D), F32),
        compiler_params=pltpu.CompilerParams(
            dimension_semantics=("arbitrary",),
            vmem_limit_bytes=V7X_VMEM_LIMIT_BYTES),
        name="mid0",
    )(x, a, b, w_out, g, w_up, w_down)


def _conv_rows(gbuf, dww_ref, ybuf, base):
    lead = CONV_HALO - (CONV_KERNEL - 1)
    nwin = CONV_ROWS + CONV_HALO
    for lt in range(CONV_WIDTH // LANES):
        lanes = slice(lt * LANES, (lt + 1) * LANES)
        win = gbuf[pl.ds(base, nwin), lanes]
        acc = None
        for r in range(SUBLANES):
            shifted = win if r == 0 else pltpu.roll(win, nwin - r, 0)
            for a in range(CONV_HALO // SUBLANES + 1):
                k = SUBLANES * a + r - lead
                if 0 <= k < CONV_KERNEL:
                    term = dww_ref[k:k + 1, lanes] * shifted[SUBLANES * a:SUBLANES * a + CONV_ROWS, :]
                    acc = term if acc is None else acc + term
        ybuf[pl.ds(base, CONV_ROWS), lanes] = acc


def _layer1_kernel(x_ref, gmix_ref, pw1w_ref, pw1b_ref, dww_ref, dwb_ref, lng_ref, lnb_ref,
                   pw2w_ref, pw2b_ref, gmlp_ref, wup_ref, wdn_ref, gfin_ref, o_ref,
                   gbuf, ybuf, hbuf, hpbuf, accbuf):
    b = pl.program_id(0)
    t = pl.program_id(1)
    T = x_ref.shape[0]

    @pl.when(jnp.logical_and(b == 0, t == 0))
    def _():
        hbuf[...] = jnp.zeros(hbuf.shape, F32)

    @pl.when(t == 0)
    def _():
        gbuf[0:CONV_HALO, :] = jnp.zeros((CONV_HALO, CONV_WIDTH), F32)

    R = T // LAYER1_ROW_BLOCKS
    row_blocks = [slice(rb * R, (rb + 1) * R) for rb in range(LAYER1_ROW_BLOCKS)]
    gates = []
    for rows in row_blocks:
        hn = _rms(x_ref[rows, :], gmix_ref[...], RMS_EPS).astype(BF16)
        gates.append(jnp.dot(hn, pw1w_ref[...], preferred_element_type=F32) + pw1b_ref[...])
    for rb, a in enumerate(gates):
        g0 = CONV_HALO + rb * R
        gbuf[g0:g0 + R, :] = a[:, :CONV_WIDTH] * jax.nn.sigmoid(a[:, CONV_WIDTH:])

    hpbuf[...] = _rms(hbuf[...], gmlp_ref[...], RMS_EPS).astype(BF16)
    accbuf[...] = jnp.zeros(accbuf.shape, F32)
    n_ff = wup_ref.shape[0]
    rows_per_ff = T // n_ff

    def ff_step(c, carry):
        up = jnp.dot(hpbuf[...], wup_ref[c], preferred_element_type=F32)
        r = jnp.maximum(up, 0.0)
        accbuf[...] += jnp.dot((r * r).astype(BF16), wdn_ref[c], preferred_element_type=F32)
        base = pl.multiple_of(c * rows_per_ff, rows_per_ff)
        for sub in range(rows_per_ff // CONV_ROWS):
            _conv_rows(gbuf, dww_ref, ybuf, base + sub * CONV_ROWS)
        return carry

    lax.fori_loop(0, n_ff, ff_step, 0)
    o_ref[...] = _rms(hbuf[...] + accbuf[...], gfin_ref[...], RMS_EPS)

    gbuf[0:CONV_HALO, :] = gbuf[T:T + CONV_HALO, :]
    outs = []
    for rows in row_blocks:
        y = ybuf[rows, :] + dwb_ref[...]
        mu = jnp.mean(y, axis=-1, keepdims=True)
        yc = y - mu
        var = jnp.mean(yc * yc, axis=-1, keepdims=True)
        y = yc * lax.rsqrt(var + LN_EPS) * lng_ref[...] + lnb_ref[...]
        y = y * jax.nn.sigmoid(y)
        outs.append(jnp.dot(y.astype(BF16), pw2w_ref[...], preferred_element_type=F32)
                    + pw2b_ref[...])
    for rows, c in zip(row_blocks, outs):
        hbuf[rows, :] = x_ref[rows, :] + c


def _layer1(x, gmix, pw1w, pw1b, dww, dwb, lng, lnb, pw2w, pw2b, gmlp, w_up, w_down, gfin):
    B, S, D = x.shape
    T = TOK_TILE
    NT = S // T
    n_ff = D_FF // FF_LOOP_CHUNK
    assert T % (n_ff * CONV_ROWS) == 0
    w_up = w_up.reshape(D, n_ff, FF_LOOP_CHUNK).transpose(1, 0, 2)
    w_down = w_down.reshape(n_ff, FF_LOOP_CHUNK, D)
    tok_in = pl.BlockSpec((None, T, D), lambda b, t: (b, jnp.minimum(t, NT - 1), 0))
    tok_out = pl.BlockSpec((None, T, D), lambda b, t: (b, jnp.maximum(t - 1, 0), 0))
    vec = lambda w: _const_spec((1, w))
    return pl.pallas_call(
        _layer1_kernel,
        grid=(B, NT + 1),
        in_specs=[tok_in, vec(D), _const_spec((D, 2 * CONV_WIDTH)), vec(2 * CONV_WIDTH),
                  _const_spec((CONV_KERNEL, CONV_WIDTH)), vec(CONV_WIDTH), vec(CONV_WIDTH),
                  vec(CONV_WIDTH), _const_spec((CONV_WIDTH, D)), vec(D), vec(D),
                  _const_spec((n_ff, D, FF_LOOP_CHUNK)), _const_spec((n_ff, FF_LOOP_CHUNK, D)),
                  vec(D)],
        out_specs=tok_out,
        out_shape=jax.ShapeDtypeStruct((B, S, D), F32),
        scratch_shapes=[pltpu.VMEM((CONV_HALO + T, CONV_WIDTH), F32),
                        pltpu.VMEM((T, CONV_WIDTH), F32),
                        pltpu.VMEM((T, D), F32),
                        pltpu.VMEM((T, D), BF16),
                        pltpu.VMEM((T, D), F32)],
        compiler_params=pltpu.CompilerParams(
            dimension_semantics=("arbitrary", "arbitrary"),
            vmem_limit_bytes=V7X_VMEM_LIMIT_BYTES),
        name="layer1",
    )(x, gmix, pw1w, pw1b, dww, dwb, lng, lnb, pw2w, pw2b, gmlp, w_up, w_down, gfin)


def _rope_lane_tables(S):
    half = ROT_DIM // 2
    pos = jnp.arange(S, dtype=F32)
    inv_freq = ROPE_THETA ** (-jnp.arange(0, ROT_DIM, 2, dtype=F32) / ROT_DIM)
    ang = pos[:, None] * inv_freq[None, :]
    cos, sin = jnp.cos(ang), jnp.sin(ang)
    rest = DA_HEAD_QK - ROT_DIM
    one = jnp.ones((S, rest), F32)
    zero = jnp.zeros((S, rest), F32)
    zh = jnp.zeros((S, half), F32)
    rc = jnp.concatenate([cos, cos, one], axis=-1)
    rs1 = jnp.concatenate([-sin, zh, zero], axis=-1)
    rs2 = jnp.concatenate([zh, sin, zero], axis=-1)
    rep = LANES // DA_HEAD_QK
    return tuple(jnp.tile(tbl, (1, rep)) for tbl in (rc, rs1, rs2))


def kernel(x, mix_norm, mlp_norm, w_up, w_down, final_norm, w_in, pool_w, pool_scale,
           lam_q1, lam_k1, lam_q2, lam_k2, subln, w_out, conv_pw1_w, conv_pw1_b,
           conv_dw_w, conv_dw_b, conv_ln_g, conv_ln_b, conv_pw2_w, conv_pw2_b):
    B, S, D = x.shape
    row = lambda v: v.reshape(1, -1)
    rc, rs1, rs2 = _rope_lane_tables(S)

    a, qT, k, vT = _front0(x, row(mix_norm[0]), w_in[0].astype(BF16), pool_w[0].astype(BF16),
                         row(pool_scale[0]), rc, rs1, rs2)
    lamv = jnp.stack([lam_q1[0], lam_k1[0], lam_q2[0], lam_k2[0]]).astype(F32)
    lambda_init = 0.8 - 0.6 * math.exp(-0.3 * 0)
    b = _attn(qT, k, vT, lamv, subln[0].reshape(-1, 1), lambda_init)
    h = _mid0(x.reshape(B * S, D), a.reshape(B * S, POOL_WIDTH), b.reshape(B * S, ATTN_WIDTH),
              w_out[0].astype(BF16), row(mlp_norm[0]), w_up[0].astype(BF16),
              w_down[0].astype(BF16))

    return _layer1(h.reshape(B, S, D), row(mix_norm[1]), conv_pw1_w[0].astype(BF16),
                   row(conv_pw1_b[0]), conv_dw_w[0], row(conv_dw_b[0]), row(conv_ln_g[0]),
                   row(conv_ln_b[0]), conv_pw2_w[0].astype(BF16), row(conv_pw2_b[0]),
                   row(mlp_norm[1]), w_up[1].astype(BF16), w_down[1].astype(BF16),
                   row(final_norm))
```

```python
import functools
import math

import jax
import jax.numpy as jnp
from jax import lax
from jax.experimental import pallas as pl
from jax.experimental.pallas import tpu as pltpu

F32 = jnp.float32
BF16 = jnp.bfloat16

D_MODEL = 1024
POOL_WIDTH = 512
POOL_WINDOWS = (2, 4, 8, 16)
POOL_GROUP = 128
ATTN_WIDTH = 512
DA_HEAD_V = 128
DA_HEADS = 4
DA_HEAD_QK = 64
ROT_DIM = 16
ROPE_THETA = 500000.0
IN_WIDTH = 2048
CONV_WIDTH = 1024
CONV_KERNEL = 31
D_FF = 4096
RMS_EPS = 1e-6
LN_EPS = 1e-5
SUBLN_EPS = 1e-5

V7X_VMEM_LIMIT_BYTES = 56 * 1024 * 1024
SUBLANES = 8
LANES = 128

TOK_TILE = 512
ATT_TILE = 512
FF_CHUNK = 1024
FRONT_ROW_BLOCKS = 2
LAYER1_ROW_BLOCKS = 2
POOL_HALO = 16
CONV_HALO = 32
CONV_ROWS = 32


def _rms(x, g, eps):
    ms = jnp.mean(x * x, axis=-1, keepdims=True)
    return x * lax.rsqrt(ms + eps) * g


def _const_spec(shape):
    nd = len(shape)
    return pl.BlockSpec(shape, lambda *_: (0,) * nd, pipeline_mode=pl.Buffered(1))


def _mlp(h, g, wup_ref, wdn_ref):
    hn = _rms(h, g, RMS_EPS).astype(BF16)
    acc = None
    for c in range(D_FF // FF_CHUNK):
        cols = slice(c * FF_CHUNK, (c + 1) * FF_CHUNK)
        up = jnp.dot(hn, wup_ref[:, cols], preferred_element_type=F32)
        r = jnp.maximum(up, 0.0)
        d = jnp.dot((r * r).astype(BF16), wdn_ref[cols, :], preferred_element_type=F32)
        acc = d if acc is None else acc + d
    return acc


def _front0_kernel(x_ref, g_ref, win_ref, pw_ref, ps_ref, rc_ref, rs1_ref, rs2_ref,
                   a_ref, qT_ref, k_ref, vT_ref, ubuf):
    t = pl.program_id(1)
    T = x_ref.shape[0]
    R = T // FRONT_ROW_BLOCKS
    scale = DA_HEAD_QK ** -0.5 * math.log2(math.e)

    @pl.when(t == 0)
    def _():
        ubuf[0:POOL_HALO, :] = jnp.zeros((POOL_HALO, POOL_WIDTH), F32)

    zs = []
    for rb in range(FRONT_ROW_BLOCKS):
        hn = _rms(x_ref[rb * R:(rb + 1) * R, :], g_ref[...], RMS_EPS).astype(BF16)
        zs.append(jnp.dot(hn, win_ref[...], preferred_element_type=F32))

    for rb, z in enumerate(zs):
        rows = slice(rb * R, (rb + 1) * R)
        u0 = POOL_HALO + rb * R
        ubuf[u0:u0 + R, :] = z[:, :POOL_WIDTH]
        pos = t * T + rb * R + lax.broadcasted_iota(jnp.int32, (R, 1), 0)
        for g, w in enumerate(POOL_WINDOWS):
            lanes = slice(g * POOL_GROUP, (g + 1) * POOL_GROUP)
            ug = ubuf[u0:u0 + R, lanes]
            acc = ug
            for j in range(1, w):
                acc = acc + ubuf[u0 - j:u0 - j + R, lanes]
            cnt = jnp.minimum(pos + 1, w).astype(F32)
            pooled = acc / cnt - ug
            ag = jnp.dot(pooled.astype(BF16), pw_ref[g], preferred_element_type=F32)
            a_ref[rows, lanes] = (ag * ps_ref[:, lanes]).astype(BF16)

        rc = rc_ref[rows, :]
        rs1 = rs1_ref[rows, :]
        rs2 = rs2_ref[rows, :]

        def rope(xh):
            return (xh * rc + pltpu.roll(xh, LANES - ROT_DIM // 2, 1) * rs1
                    + pltpu.roll(xh, ROT_DIM // 2, 1) * rs2)

        for h in range(DA_HEADS):
            lanes = slice(h * LANES, (h + 1) * LANES)
            qh = rope(z[:, POOL_WIDTH + h * LANES: POOL_WIDTH + (h + 1) * LANES]) * scale
            qT_ref[lanes, rows] = qh.T.astype(BF16)
            off = POOL_WIDTH + ATTN_WIDTH
            k_ref[rows, lanes] = rope(z[:, off + h * LANES: off + (h + 1) * LANES]).astype(BF16)
            off = POOL_WIDTH + 2 * ATTN_WIDTH
            vT_ref[lanes, rows] = z[:, off + h * LANES: off + (h + 1) * LANES].T.astype(BF16)
    ubuf[0:POOL_HALO, :] = ubuf[T:T + POOL_HALO, :]


def _front0(x, g, w_in, pool_w, pool_scale, rc, rs1, rs2):
    B, S, D = x.shape
    T = TOK_TILE
    out = jax.ShapeDtypeStruct((B, S, ATTN_WIDTH), BF16)
    outT = jax.ShapeDtypeStruct((B, S // T, ATTN_WIDTH, T), BF16)
    tok = lambda w: pl.BlockSpec((None, T, w), lambda b, t: (b, t, 0))
    tokT = pl.BlockSpec((None, None, ATTN_WIDTH, T), lambda b, t: (b, t, 0, 0))
    rope = pl.BlockSpec((T, LANES), lambda b, t: (t, 0))
    return pl.pallas_call(
        _front0_kernel,
        grid=(B, S // T),
        in_specs=[tok(D), _const_spec((1, D)), _const_spec((D, IN_WIDTH)),
                  _const_spec((len(POOL_WINDOWS), POOL_GROUP, POOL_GROUP)),
                  _const_spec((1, POOL_WIDTH)), rope, rope, rope],
        out_specs=[tok(POOL_WIDTH), tokT, tok(ATTN_WIDTH), tokT],
        out_shape=[out, outT, out, outT],
        scratch_shapes=[pltpu.VMEM((POOL_HALO + T, POOL_WIDTH), F32)],
        compiler_params=pltpu.CompilerParams(
            dimension_semantics=("arbitrary", "arbitrary"),
            vmem_limit_bytes=V7X_VMEM_LIMIT_BYTES),
        name="front0",
    )(x, g, w_in, pool_w, pool_scale, rc, rs1, rs2)


def _attn_kernel(lam_ref, subln_ref, qT_ref, qTn_ref, k_ref, vT_ref, o_ref, acc1, acc2,
                 s1buf, s2buf, qbuf, mxbuf, *, lambda_init):
    i = pl.program_id(2)
    TQ = qT_ref.shape[1]
    acc1[...] = jnp.zeros(acc1.shape, F32)
    acc2[...] = jnp.zeros(acc2.shape, F32)
    comps = ((0, s1buf, acc1), (1, s2buf, acc2))

    def set_q(src_ref):
        qT = src_ref[...]
        feat = lax.broadcasted_iota(jnp.int32, qT.shape, 0)
        zero = jnp.zeros_like(qT)
        qbuf[0] = jnp.where(feat < DA_HEAD_QK, qT, zero)
        qbuf[1] = jnp.where(feat >= DA_HEAD_QK, qT, zero)

    def scores(j, c, sbuf, masked):
        sT = jnp.dot(k_ref[j], qbuf[c], preferred_element_type=F32)
        if masked:
            key = lax.broadcasted_iota(jnp.int32, sT.shape, 0)
            qry = lax.broadcasted_iota(jnp.int32, sT.shape, 1)
            sT = jnp.where(key <= qry, sT, -jnp.inf)
        sbuf[...] = sT
        return jnp.max(sT, axis=0, keepdims=True)

    def step(j, carry, nxt):
        if nxt == "next_q":
            set_q(qTn_ref)
        out = []
        for (c, sbuf, acc), (m_old, l_old, mx) in zip(comps, carry):
            m_new = jnp.maximum(m_old, mx)
            alpha = jnp.exp2(m_old - m_new)
            p = jnp.exp2(sbuf[...] - m_new)
            l_new = alpha * l_old + jnp.sum(p, axis=0, keepdims=True)
            p = p.astype(BF16)
            if nxt == "next_q":
                mxbuf[c] = scores(0, c, sbuf, False)
            elif nxt is not None:
                mx = scores(j + 1, c, sbuf, nxt)
            acc[...] = alpha * acc[...] + jnp.dot(vT_ref[j], p, preferred_element_type=F32)
            out.append((m_new, l_new, mx))
        return tuple(out)

    @pl.when(i == 0)
    def _():
        set_q(qT_ref)
        for c, sbuf, _ in comps:
            mxbuf[c] = scores(0, c, sbuf, True)

    m0 = jnp.full((1, TQ), -jnp.inf, F32)
    l0 = jnp.zeros((1, TQ), F32)
    carry = ((m0, l0, mxbuf[0]), (m0, l0, mxbuf[1]))
    n_plain = jnp.maximum(i - 1, 0)
    odd = n_plain % 2
    carry = lax.cond(odd == 1, lambda cr: step(0, cr, False), lambda cr: cr, carry)
    carry = lax.fori_loop(
        0, n_plain // 2,
        lambda jj, cr: step(odd + 2 * jj + 1, step(odd + 2 * jj, cr, False), False), carry)
    carry = lax.cond(i > 0, lambda cr: step(i - 1, cr, True), lambda cr: cr, carry)
    (_, l1, _), (_, l2, _) = step(i, carry, "next_q")

    lamv = lam_ref[...]
    lam = (jnp.exp(jnp.sum(lamv[0:1] * lamv[1:2], axis=-1, keepdims=True))
           - jnp.exp(jnp.sum(lamv[2:3] * lamv[3:4], axis=-1, keepdims=True))
           + lambda_init)
    o = acc1[...] / l1 - lam * (acc2[...] / l2)
    ms = jnp.mean(o * o, axis=0, keepdims=True)
    o = o * lax.rsqrt(ms + SUBLN_EPS) * subln_ref[...] * (1.0 - lambda_init)
    o_ref[...] = o.T.astype(o_ref.dtype)


def _attn(qT, k, vT, lamv, subln_col, lambda_init):
    B, NT, _, TQ = qT.shape
    S = NT * TQ
    k = k.reshape(B, NT, TQ, ATTN_WIDTH)
    accs = pltpu.VMEM((DA_HEAD_V, TQ), F32)
    return pl.pallas_call(
        functools.partial(_attn_kernel, lambda_init=lambda_init),
        grid=(B, DA_HEADS, NT),
        in_specs=[_const_spec((4, DA_HEAD_QK)), _const_spec((DA_HEAD_V, 1)),
                  pl.BlockSpec((None, None, LANES, TQ), lambda b, h, i: (b, i, h, 0)),
                  pl.BlockSpec((None, None, LANES, TQ),
                               lambda b, h, i: (b, jnp.minimum(i + 1, NT - 1), h, 0)),
                  pl.BlockSpec((None, NT, TQ, LANES), lambda b, h, i: (b, 0, 0, h)),
                  pl.BlockSpec((None, NT, LANES, TQ), lambda b, h, i: (b, 0, h, 0))],
        out_specs=pl.BlockSpec((None, TQ, LANES), lambda b, h, i: (b, i, h)),
        out_shape=jax.ShapeDtypeStruct((B, S, ATTN_WIDTH), BF16),
        scratch_shapes=[accs, accs, pltpu.VMEM((TQ, TQ), F32), pltpu.VMEM((TQ, TQ), F32),
                        pltpu.VMEM((2, LANES, TQ), BF16), pltpu.VMEM((2, 1, TQ), F32)],
        compiler_params=pltpu.CompilerParams(
            dimension_semantics=("arbitrary", "arbitrary", "arbitrary"),
            vmem_limit_bytes=V7X_VMEM_LIMIT_BYTES),
        name="attn",
    )(lamv, subln_col, qT, qT, k, vT)


def _mid0_kernel(x_ref, a_ref, b_ref, wo_ref, g_ref, wup_ref, wdn_ref, o_ref):
    mix = (jnp.dot(a_ref[...], wo_ref[0:POOL_WIDTH, :], preferred_element_type=F32)
           + jnp.dot(b_ref[...], wo_ref[POOL_WIDTH:, :], preferred_element_type=F32))
    h = x_ref[...] + mix
    o_ref[...] = h + _mlp(h, g_ref[...], wup_ref, wdn_ref)


def _mid0(x, a, b, w_out, g, w_up, w_down):
    N, D = x.shape
    T = TOK_TILE
    tok = lambda w: pl.BlockSpec((T, w), lambda t: (t, 0))
    return pl.pallas_call(
        _mid0_kernel,
        grid=(N // T,),
        in_specs=[tok(D), tok(POOL_WIDTH), tok(ATTN_WIDTH), _const_spec((D, D)),
                  _const_spec((1, D)), _const_spec((D, D_FF)), _const_spec((D_FF, D))],
        out_specs=tok(D),
        out_shape=jax.ShapeDtypeStruct((N, D), F32),
        compiler_params=pltpu.CompilerParams(
            dimension_semantics=("arbitrary",),
            vmem_limit_bytes=V7X_VMEM_LIMIT_BYTES),
        name="mid0",
    )(x, a, b, w_out, g, w_up, w_down)


def _conv_rows(gbuf, dww_ref, ybuf, base):
    lead = CONV_HALO - (CONV_KERNEL - 1)
    nwin = CONV_ROWS + CONV_HALO
    for lt in range(CONV_WIDTH // LANES):
        lanes = slice(lt * LANES, (lt + 1) * LANES)
        win = gbuf[pl.ds(base, nwin), lanes]
        acc = None
        for r in range(SUBLANES):
            shifted = win if r == 0 else pltpu.roll(win, nwin - r, 0)
            for a in range(CONV_HALO // SUBLANES + 1):
                k = SUBLANES * a + r - lead
                if 0 <= k < CONV_KERNEL:
                    term = dww_ref[k:k + 1, lanes] * shifted[SUBLANES * a:SUBLANES * a + CONV_ROWS, :]
                    acc = term if acc is None else acc + term
        ybuf[pl.ds(base, CONV_ROWS), lanes] = acc


def _conv1_kernel(x_ref, gmix_ref, pw1w_ref, pw1b_ref, dww_ref, dwb_ref, lng_ref, lnb_ref,
                  pw2w_ref, pw2b_ref, o_ref, gbuf, ybuf):
    t = pl.program_id(1)
    T = x_ref.shape[0]

    @pl.when(t == 0)
    def _():
        gbuf[0:CONV_HALO, :] = jnp.zeros((CONV_HALO, CONV_WIDTH), F32)

    R = T // LAYER1_ROW_BLOCKS
    row_blocks = [slice(rb * R, (rb + 1) * R) for rb in range(LAYER1_ROW_BLOCKS)]
    gates = []
    for rows in row_blocks:
        hn = _rms(x_ref[rows, :], gmix_ref[...], RMS_EPS).astype(BF16)
        gates.append(jnp.dot(hn, pw1w_ref[...], preferred_element_type=F32) + pw1b_ref[...])
    for rb, a in enumerate(gates):
        g0 = CONV_HALO + rb * R
        gbuf[g0:g0 + R, :] = a[:, :CONV_WIDTH] * jax.nn.sigmoid(a[:, CONV_WIDTH:])

    def conv_step(c, carry):
        _conv_rows(gbuf, dww_ref, ybuf, pl.multiple_of(c * CONV_ROWS, CONV_ROWS))
        return carry

    lax.fori_loop(0, T // CONV_ROWS, conv_step, 0)
    gbuf[0:CONV_HALO, :] = gbuf[T:T + CONV_HALO, :]

    outs = []
    for rows in row_blocks:
        y = ybuf[rows, :] + dwb_ref[...]
        mu = jnp.mean(y, axis=-1, keepdims=True)
        yc = y - mu
        var = jnp.mean(yc * yc, axis=-1, keepdims=True)
        y = yc * lax.rsqrt(var + LN_EPS) * lng_ref[...] + lnb_ref[...]
        y = y * jax.nn.sigmoid(y)
        outs.append(jnp.dot(y.astype(BF16), pw2w_ref[...], preferred_element_type=F32)
                    + pw2b_ref[...])
    for rows, c in zip(row_blocks, outs):
        o_ref[rows, :] = x_ref[rows, :] + c


def _conv1(x, gmix, pw1w, pw1b, dww, dwb, lng, lnb, pw2w, pw2b):
    B, S, D = x.shape
    T = TOK_TILE
    assert T % CONV_ROWS == 0
    tok = pl.BlockSpec((None, T, D), lambda b, t: (b, t, 0))
    vec = lambda w: _const_spec((1, w))
    return pl.pallas_call(
        _conv1_kernel,
        grid=(B, S // T),
        in_specs=[tok, vec(D), _const_spec((D, 2 * CONV_WIDTH)), vec(2 * CONV_WIDTH),
                  _const_spec((CONV_KERNEL, CONV_WIDTH)), vec(CONV_WIDTH), vec(CONV_WIDTH),
                  vec(CONV_WIDTH), _const_spec((CONV_WIDTH, D)), vec(D)],
        out_specs=tok,
        out_shape=jax.ShapeDtypeStruct((B, S, D), F32),
        scratch_shapes=[pltpu.VMEM((CONV_HALO + T, CONV_WIDTH), F32),
                        pltpu.VMEM((T, CONV_WIDTH), F32)],
        compiler_params=pltpu.CompilerParams(
            dimension_semantics=("arbitrary", "arbitrary"),
            vmem_limit_bytes=V7X_VMEM_LIMIT_BYTES),
        name="conv1",
    )(x, gmix, pw1w, pw1b, dww, dwb, lng, lnb, pw2w, pw2b)


def _mlp1_kernel(h_ref, g_ref, wup_ref, wdn_ref, gfin_ref, o_ref):
    h = h_ref[...]
    o_ref[...] = _rms(h + _mlp(h, g_ref[...], wup_ref, wdn_ref), gfin_ref[...], RMS_EPS)


def _mlp1(h, g, w_up, w_down, gfin):
    N, D = h.shape
    T = TOK_TILE
    tok = pl.BlockSpec((T, D), lambda t: (t, 0))
    return pl.pallas_call(
        _mlp1_kernel,
        grid=(N // T,),
        in_specs=[tok, _const_spec((1, D)), _const_spec((D, D_FF)), _const_spec((D_FF, D)),
                  _const_spec((1, D))],
        out_specs=tok,
        out_shape=jax.ShapeDtypeStruct((N, D), F32),
        compiler_params=pltpu.CompilerParams(
            dimension_semantics=("arbitrary",),
            vmem_limit_bytes=V7X_VMEM_LIMIT_BYTES),
        name="mlp1",
    )(h, g, w_up, w_down, gfin)


def _rope_lane_tables(S):
    half = ROT_DIM // 2
    pos = jnp.arange(S, dtype=F32)
    inv_freq = ROPE_THETA ** (-jnp.arange(0, ROT_DIM, 2, dtype=F32) / ROT_DIM)
    ang = pos[:, None] * inv_freq[None, :]
    cos, sin = jnp.cos(ang), jnp.sin(ang)
    rest = DA_HEAD_QK - ROT_DIM
    one = jnp.ones((S, rest), F32)
    zero = jnp.zeros((S, rest), F32)
    zh = jnp.zeros((S, half), F32)
    rc = jnp.concatenate([cos, cos, one], axis=-1)
    rs1 = jnp.concatenate([-sin, zh, zero], axis=-1)
    rs2 = jnp.concatenate([zh, sin, zero], axis=-1)
    rep = LANES // DA_HEAD_QK
    return tuple(jnp.tile(tbl, (1, rep)) for tbl in (rc, rs1, rs2))


def kernel(x, mix_norm, mlp_norm, w_up, w_down, final_norm, w_in, pool_w, pool_scale,
           lam_q1, lam_k1, lam_q2, lam_k2, subln, w_out, conv_pw1_w, conv_pw1_b,
           conv_dw_w, conv_dw_b, conv_ln_g, conv_ln_b, conv_pw2_w, conv_pw2_b):
    B, S, D = x.shape
    row = lambda v: v.reshape(1, -1)
    rc, rs1, rs2 = _rope_lane_tables(S)

    a, qT, k, vT = _front0(x, row(mix_norm[0]), w_in[0].astype(BF16), pool_w[0].astype(BF16),
                         row(pool_scale[0]), rc, rs1, rs2)
    lamv = jnp.stack([lam_q1[0], lam_k1[0], lam_q2[0], lam_k2[0]]).astype(F32)
    lambda_init = 0.8 - 0.6 * math.exp(-0.3 * 0)
    b = _attn(qT, k, vT, lamv, subln[0].reshape(-1, 1), lambda_init)
    h = _mid0(x.reshape(B * S, D), a.reshape(B * S, POOL_WIDTH), b.reshape(B * S, ATTN_WIDTH),
              w_out[0].astype(BF16), row(mlp_norm[0]), w_up[0].astype(BF16),
              w_down[0].astype(BF16))

    h = _conv1(h.reshape(B, S, D), row(mix_norm[1]), conv_pw1_w[0].astype(BF16),
               row(conv_pw1_b[0]), conv_dw_w[0], row(conv_dw_b[0]), row(conv_ln_g[0]),
               row(conv_ln_b[0]), conv_pw2_w[0].astype(BF16), row(conv_pw2_b[0]))
    out = _mlp1(h.reshape(B * S, D), row(mlp_norm[1]), w_up[1].astype(BF16),
                w_down[1].astype(BF16), row(final_norm))
    return out.reshape(B, S, D)
```

```python
import functools
import math

import jax
import jax.numpy as jnp
from jax import lax
from jax.experimental import pallas as pl
from jax.experimental.pallas import tpu as pltpu

F32 = jnp.float32
BF16 = jnp.bfloat16

D_MODEL = 1024
POOL_WIDTH = 512
POOL_WINDOWS = (2, 4, 8, 16)
POOL_GROUP = 128
ATTN_WIDTH = 512
DA_HEAD_V = 128
DA_HEADS = 4
DA_HEAD_QK = 64
ROT_DIM = 16
ROPE_THETA = 500000.0
IN_WIDTH = 2048
CONV_WIDTH = 1024
CONV_KERNEL = 31
D_FF = 4096
RMS_EPS = 1e-6
LN_EPS = 1e-5
SUBLN_EPS = 1e-5

V7X_VMEM_LIMIT_BYTES = 56 * 1024 * 1024
SUBLANES = 8
LANES = 128

TOK_TILE = 512
ATT_TILE = 512
FF_CHUNK = 1024
FRONT_ROW_BLOCKS = 2
LAYER1_ROW_BLOCKS = 2
POOL_HALO = 16
CONV_HALO = 32
CONV_ROWS = 32


def _rms(x, g, eps):
    ms = jnp.mean(x * x, axis=-1, keepdims=True)
    return x * lax.rsqrt(ms + eps) * g


def _const_spec(shape):
    nd = len(shape)
    return pl.BlockSpec(shape, lambda *_: (0,) * nd, pipeline_mode=pl.Buffered(1))


def _mlp(h, g, wup_ref, wdn_ref):
    hn = _rms(h, g, RMS_EPS).astype(BF16)
    acc = None
    for c in range(D_FF // FF_CHUNK):
        cols = slice(c * FF_CHUNK, (c + 1) * FF_CHUNK)
        up = jnp.dot(hn, wup_ref[:, cols], preferred_element_type=F32)
        r = jnp.maximum(up, 0.0)
        d = jnp.dot((r * r).astype(BF16), wdn_ref[cols, :], preferred_element_type=F32)
        acc = d if acc is None else acc + d
    return acc


def _front0_kernel(x_ref, g_ref, win_ref, pw_ref, ps_ref, rc_ref, rs1_ref, rs2_ref,
                   a_ref, qT_ref, k_ref, vT_ref, ubuf):
    t = pl.program_id(1)
    T = x_ref.shape[0]
    R = T // FRONT_ROW_BLOCKS
    scale = DA_HEAD_QK ** -0.5 * math.log2(math.e)

    @pl.when(t == 0)
    def _():
        ubuf[0:POOL_HALO, :] = jnp.zeros((POOL_HALO, POOL_WIDTH), F32)

    zs = []
    for rb in range(FRONT_ROW_BLOCKS):
        hn = _rms(x_ref[rb * R:(rb + 1) * R, :], g_ref[...], RMS_EPS).astype(BF16)
        zs.append(jnp.dot(hn, win_ref[...], preferred_element_type=F32))

    for rb, z in enumerate(zs):
        rows = slice(rb * R, (rb + 1) * R)
        u0 = POOL_HALO + rb * R
        ubuf[u0:u0 + R, :] = z[:, :POOL_WIDTH]
        pos = t * T + rb * R + lax.broadcasted_iota(jnp.int32, (R, 1), 0)
        for g, w in enumerate(POOL_WINDOWS):
            lanes = slice(g * POOL_GROUP, (g + 1) * POOL_GROUP)
            ug = ubuf[u0:u0 + R, lanes]
            acc = ug
            for j in range(1, w):
                acc = acc + ubuf[u0 - j:u0 - j + R, lanes]
            cnt = jnp.minimum(pos + 1, w).astype(F32)
            pooled = acc / cnt - ug
            ag = jnp.dot(pooled.astype(BF16), pw_ref[g], preferred_element_type=F32)
            a_ref[rows, lanes] = (ag * ps_ref[:, lanes]).astype(BF16)

        rc = rc_ref[rows, :]
        rs1 = rs1_ref[rows, :]
        rs2 = rs2_ref[rows, :]

        def rope(xh):
            return (xh * rc + pltpu.roll(xh, LANES - ROT_DIM // 2, 1) * rs1
                    + pltpu.roll(xh, ROT_DIM // 2, 1) * rs2)

        for h in range(DA_HEADS):
            lanes = slice(h * LANES, (h + 1) * LANES)
            qh = rope(z[:, POOL_WIDTH + h * LANES: POOL_WIDTH + (h + 1) * LANES]) * scale
            qT_ref[lanes, rows] = qh.T.astype(BF16)
            off = POOL_WIDTH + ATTN_WIDTH
            k_ref[rows, lanes] = rope(z[:, off + h * LANES: off + (h + 1) * LANES]).astype(BF16)
            off = POOL_WIDTH + 2 * ATTN_WIDTH
            vT_ref[lanes, rows] = z[:, off + h * LANES: off + (h + 1) * LANES].T.astype(BF16)
    ubuf[0:POOL_HALO, :] = ubuf[T:T + POOL_HALO, :]


def _front0(x, g, w_in, pool_w, pool_scale, rc, rs1, rs2):
    B, S, D = x.shape
    T = TOK_TILE
    out = jax.ShapeDtypeStruct((B, S, ATTN_WIDTH), BF16)
    outT = jax.ShapeDtypeStruct((B, S // T, ATTN_WIDTH, T), BF16)
    tok = lambda w: pl.BlockSpec((None, T, w), lambda b, t: (b, t, 0))
    tokT = pl.BlockSpec((None, None, ATTN_WIDTH, T), lambda b, t: (b, t, 0, 0))
    rope = pl.BlockSpec((T, LANES), lambda b, t: (t, 0))
    return pl.pallas_call(
        _front0_kernel,
        grid=(B, S // T),
        in_specs=[tok(D), _const_spec((1, D)), _const_spec((D, IN_WIDTH)),
                  _const_spec((len(POOL_WINDOWS), POOL_GROUP, POOL_GROUP)),
                  _const_spec((1, POOL_WIDTH)), rope, rope, rope],
        out_specs=[tok(POOL_WIDTH), tokT, tok(ATTN_WIDTH), tokT],
        out_shape=[out, outT, out, outT],
        scratch_shapes=[pltpu.VMEM((POOL_HALO + T, POOL_WIDTH), F32)],
        compiler_params=pltpu.CompilerParams(
            dimension_semantics=("arbitrary", "arbitrary"),
            vmem_limit_bytes=V7X_VMEM_LIMIT_BYTES),
        name="front0",
    )(x, g, w_in, pool_w, pool_scale, rc, rs1, rs2)


def _attn_kernel(lam_ref, subln_ref, qT_ref, qTn_ref, k_ref, vT_ref, o_ref, acc1, acc2,
                 s1buf, s2buf, qbuf, qnbuf, mxbuf, *, lambda_init):
    i = pl.program_id(2)
    TQ = qT_ref.shape[1]
    acc1[...] = jnp.zeros(acc1.shape, F32)
    acc2[...] = jnp.zeros(acc2.shape, F32)
    comps = ((0, s1buf, acc1), (1, s2buf, acc2))

    def set_q(src_ref, dst):
        qT = src_ref[...]
        feat = lax.broadcasted_iota(jnp.int32, qT.shape, 0)
        zero = jnp.zeros_like(qT)
        dst[0] = jnp.where(feat < DA_HEAD_QK, qT, zero)
        dst[1] = jnp.where(feat >= DA_HEAD_QK, qT, zero)

    set_q(qT_ref, qbuf)

    def scores(j, c, sbuf, masked, q=qbuf):
        sT = jnp.dot(k_ref[j], q[c], preferred_element_type=F32)
        if masked:
            key = lax.broadcasted_iota(jnp.int32, sT.shape, 0)
            qry = lax.broadcasted_iota(jnp.int32, sT.shape, 1)
            sT = jnp.where(key <= qry, sT, -jnp.inf)
        sbuf[...] = sT
        return jnp.max(sT, axis=0, keepdims=True)

    def step(j, carry, nxt):
        if nxt == "next_q":
            set_q(qTn_ref, qnbuf)
        out = []
        for (c, sbuf, acc), (m_old, l_old, mx) in zip(comps, carry):
            m_new = jnp.maximum(m_old, mx)
            alpha = jnp.exp2(m_old - m_new)
            p = jnp.exp2(sbuf[...] - m_new)
            l_new = alpha * l_old + jnp.sum(p, axis=0, keepdims=True)
            p = p.astype(BF16)
            if nxt == "next_q":
                mxbuf[c] = scores(0, c, sbuf, False, q=qnbuf)
            elif nxt is not None:
                mx = scores(j + 1, c, sbuf, nxt)
            acc[...] = alpha * acc[...] + jnp.dot(vT_ref[j], p, preferred_element_type=F32)
            out.append((m_new, l_new, mx))
        return tuple(out)

    @pl.when(i == 0)
    def _():
        for c, sbuf, _ in comps:
            mxbuf[c] = scores(0, c, sbuf, True)

    m0 = jnp.full((1, TQ), -jnp.inf, F32)
    l0 = jnp.zeros((1, TQ), F32)
    carry = ((m0, l0, mxbuf[0]), (m0, l0, mxbuf[1]))
    n_plain = jnp.maximum(i - 1, 0)
    odd = n_plain % 2
    carry = lax.cond(odd == 1, lambda cr: step(0, cr, False), lambda cr: cr, carry)
    carry = lax.fori_loop(
        0, n_plain // 2,
        lambda jj, cr: step(odd + 2 * jj + 1, step(odd + 2 * jj, cr, False), False), carry)
    def finish(cr):
        (_, l1, _), (_, l2, _) = step(i, cr, "next_q")
        lamv = lam_ref[...]
        lam = (jnp.exp(jnp.sum(lamv[0:1] * lamv[1:2], axis=-1, keepdims=True))
               - jnp.exp(jnp.sum(lamv[2:3] * lamv[3:4], axis=-1, keepdims=True))
               + lambda_init)
        o = acc1[...] / l1 - lam * (acc2[...] / l2)
        ms = jnp.mean(o * o, axis=0, keepdims=True)
        o = o * lax.rsqrt(ms + SUBLN_EPS) * subln_ref[...] * (1.0 - lambda_init)
        o_ref[...] = o.T.astype(o_ref.dtype)

    @pl.when(i > 0)
    def _():
        finish(step(i - 1, carry, True))

    @pl.when(i == 0)
    def _():
        finish(carry)


def _attn(qT, k, vT, lamv, subln_col, lambda_init):
    B, NT, _, TQ = qT.shape
    S = NT * TQ
    k = k.reshape(B, NT, TQ, ATTN_WIDTH)
    accs = pltpu.VMEM((DA_HEAD_V, TQ), F32)
    return pl.pallas_call(
        functools.partial(_attn_kernel, lambda_init=lambda_init),
        grid=(B, DA_HEADS, NT),
        in_specs=[_const_spec((4, DA_HEAD_QK)), _const_spec((DA_HEAD_V, 1)),
                  pl.BlockSpec((None, None, LANES, TQ), lambda b, h, i: (b, i, h, 0)),
                  pl.BlockSpec((None, None, LANES, TQ),
                               lambda b, h, i: (b, jnp.minimum(i + 1, NT - 1), h, 0)),
                  pl.BlockSpec((None, NT, TQ, LANES), lambda b, h, i: (b, 0, 0, h)),
                  pl.BlockSpec((None, NT, LANES, TQ), lambda b, h, i: (b, 0, h, 0))],
        out_specs=pl.BlockSpec((None, TQ, LANES), lambda b, h, i: (b, i, h)),
        out_shape=jax.ShapeDtypeStruct((B, S, ATTN_WIDTH), BF16),
        scratch_shapes=[accs, accs, pltpu.VMEM((TQ, TQ), F32), pltpu.VMEM((TQ, TQ), F32),
                        pltpu.VMEM((2, LANES, TQ), BF16), pltpu.VMEM((2, LANES, TQ), BF16),
                        pltpu.VMEM((2, 1, TQ), F32)],
        compiler_params=pltpu.CompilerParams(
            dimension_semantics=("arbitrary", "arbitrary", "arbitrary"),
            vmem_limit_bytes=V7X_VMEM_LIMIT_BYTES),
        name="attn",
    )(lamv, subln_col, qT, qT, k, vT)


def _mid0_kernel(x_ref, a_ref, b_ref, wo_ref, g_ref, wup_ref, wdn_ref, o_ref):
    mix = (jnp.dot(a_ref[...], wo_ref[0:POOL_WIDTH, :], preferred_element_type=F32)
           + jnp.dot(b_ref[...], wo_ref[POOL_WIDTH:, :], preferred_element_type=F32))
    h = x_ref[...] + mix
    o_ref[...] = h + _mlp(h, g_ref[...], wup_ref, wdn_ref)


def _mid0(x, a, b, w_out, g, w_up, w_down):
    N, D = x.shape
    T = TOK_TILE
    tok = lambda w: pl.BlockSpec((T, w), lambda t: (t, 0))
    return pl.pallas_call(
        _mid0_kernel,
        grid=(N // T,),
        in_specs=[tok(D), tok(POOL_WIDTH), tok(ATTN_WIDTH), _const_spec((D, D)),
                  _const_spec((1, D)), _const_spec((D, D_FF)), _const_spec((D_FF, D))],
        out_specs=tok(D),
        out_shape=jax.ShapeDtypeStruct((N, D), F32),
        compiler_params=pltpu.CompilerParams(
            dimension_semantics=("arbitrary",),
            vmem_limit_bytes=V7X_VMEM_LIMIT_BYTES),
        name="mid0",
    )(x, a, b, w_out, g, w_up, w_down)


def _conv_rows(gbuf, dww_ref, ybuf, base):
    lead = CONV_HALO - (CONV_KERNEL - 1)
    nwin = CONV_ROWS + CONV_HALO
    for lt in range(CONV_WIDTH // LANES):
        lanes = slice(lt * LANES, (lt + 1) * LANES)
        win = gbuf[pl.ds(base, nwin), lanes]
        acc = None
        for r in range(SUBLANES):
            shifted = win if r == 0 else pltpu.roll(win, nwin - r, 0)
            for a in range(CONV_HALO // SUBLANES + 1):
                k = SUBLANES * a + r - lead
                if 0 <= k < CONV_KERNEL:
                    term = dww_ref[k:k + 1, lanes] * shifted[SUBLANES * a:SUBLANES * a + CONV_ROWS, :]
                    acc = term if acc is None else acc + term
        ybuf[pl.ds(base, CONV_ROWS), lanes] = acc


def _conv1_kernel(x_ref, gmix_ref, pw1w_ref, pw1b_ref, dww_ref, dwb_ref, lng_ref, lnb_ref,
                  pw2w_ref, pw2b_ref, o_ref, gbuf, ybuf):
    t = pl.program_id(1)
    T = x_ref.shape[0]

    @pl.when(t == 0)
    def _():
        gbuf[0:CONV_HALO, :] = jnp.zeros((CONV_HALO, CONV_WIDTH), F32)

    R = T // LAYER1_ROW_BLOCKS
    row_blocks = [slice(rb * R, (rb + 1) * R) for rb in range(LAYER1_ROW_BLOCKS)]
    gates = []
    for rows in row_blocks:
        hn = _rms(x_ref[rows, :], gmix_ref[...], RMS_EPS).astype(BF16)
        gates.append(jnp.dot(hn, pw1w_ref[...], preferred_element_type=F32) + pw1b_ref[...])
    for rb, a in enumerate(gates):
        g0 = CONV_HALO + rb * R
        gbuf[g0:g0 + R, :] = a[:, :CONV_WIDTH] * jax.nn.sigmoid(a[:, CONV_WIDTH:])

    def conv_step(c, carry):
        _conv_rows(gbuf, dww_ref, ybuf, pl.multiple_of(c * CONV_ROWS, CONV_ROWS))
        return carry

    lax.fori_loop(0, T // CONV_ROWS, conv_step, 0)
    gbuf[0:CONV_HALO, :] = gbuf[T:T + CONV_HALO, :]

    outs = []
    for rows in row_blocks:
        y = ybuf[rows, :] + dwb_ref[...]
        mu = jnp.mean(y, axis=-1, keepdims=True)
        yc = y - mu
        var = jnp.mean(yc * yc, axis=-1, keepdims=True)
        y = yc * lax.rsqrt(var + LN_EPS) * lng_ref[...] + lnb_ref[...]
        y = y * jax.nn.sigmoid(y)
        outs.append(jnp.dot(y.astype(BF16), pw2w_ref[...], preferred_element_type=F32)
                    + pw2b_ref[...])
    for rows, c in zip(row_blocks, outs):
        o_ref[rows, :] = x_ref[rows, :] + c


def _conv1(x, gmix, pw1w, pw1b, dww, dwb, lng, lnb, pw2w, pw2b):
    B, S, D = x.shape
    T = TOK_TILE
    assert T % CONV_ROWS == 0
    tok = pl.BlockSpec((None, T, D), lambda b, t: (b, t, 0))
    vec = lambda w: _const_spec((1, w))
    return pl.pallas_call(
        _conv1_kernel,
        grid=(B, S // T),
        in_specs=[tok, vec(D), _const_spec((D, 2 * CONV_WIDTH)), vec(2 * CONV_WIDTH),
                  _const_spec((CONV_KERNEL, CONV_WIDTH)), vec(CONV_WIDTH), vec(CONV_WIDTH),
                  vec(CONV_WIDTH), _const_spec((CONV_WIDTH, D)), vec(D)],
        out_specs=tok,
        out_shape=jax.ShapeDtypeStruct((B, S, D), F32),
        scratch_shapes=[pltpu.VMEM((CONV_HALO + T, CONV_WIDTH), F32),
                        pltpu.VMEM((T, CONV_WIDTH), F32)],
        compiler_params=pltpu.CompilerParams(
            dimension_semantics=("arbitrary", "arbitrary"),
            vmem_limit_bytes=V7X_VMEM_LIMIT_BYTES),
        name="conv1",
    )(x, gmix, pw1w, pw1b, dww, dwb, lng, lnb, pw2w, pw2b)


def _mlp1_kernel(h_ref, g_ref, wup_ref, wdn_ref, gfin_ref, o_ref):
    h = h_ref[...]
    o_ref[...] = _rms(h + _mlp(h, g_ref[...], wup_ref, wdn_ref), gfin_ref[...], RMS_EPS)


def _mlp1(h, g, w_up, w_down, gfin):
    N, D = h.shape
    T = TOK_TILE
    tok = pl.BlockSpec((T, D), lambda t: (t, 0))
    return pl.pallas_call(
        _mlp1_kernel,
        grid=(N // T,),
        in_specs=[tok, _const_spec((1, D)), _const_spec((D, D_FF)), _const_spec((D_FF, D)),
                  _const_spec((1, D))],
        out_specs=tok,
        out_shape=jax.ShapeDtypeStruct((N, D), F32),
        compiler_params=pltpu.CompilerParams(
            dimension_semantics=("arbitrary",),
            vmem_limit_bytes=V7X_VMEM_LIMIT_BYTES),
        name="mlp1",
    )(h, g, w_up, w_down, gfin)


def _rope_lane_tables(S):
    half = ROT_DIM // 2
    pos = jnp.arange(S, dtype=F32)
    inv_freq = ROPE_THETA ** (-jnp.arange(0, ROT_DIM, 2, dtype=F32) / ROT_DIM)
    ang = pos[:, None] * inv_freq[None, :]
    cos, sin = jnp.cos(ang), jnp.sin(ang)
    rest = DA_HEAD_QK - ROT_DIM
    one = jnp.ones((S, rest), F32)
    zero = jnp.zeros((S, rest), F32)
    zh = jnp.zeros((S, half), F32)
    rc = jnp.concatenate([cos, cos, one], axis=-1)
    rs1 = jnp.concatenate([-sin, zh, zero], axis=-1)
    rs2 = jnp.concatenate([zh, sin, zero], axis=-1)
    rep = LANES // DA_HEAD_QK
    return tuple(jnp.tile(tbl, (1, rep)) for tbl in (rc, rs1, rs2))


def kernel(x, mix_norm, mlp_norm, w_up, w_down, final_norm, w_in, pool_w, pool_scale,
           lam_q1, lam_k1, lam_q2, lam_k2, subln, w_out, conv_pw1_w, conv_pw1_b,
           conv_dw_w, conv_dw_b, conv_ln_g, conv_ln_b, conv_pw2_w, conv_pw2_b):
    B, S, D = x.shape
    row = lambda v: v.reshape(1, -1)
    rc, rs1, rs2 = _rope_lane_tables(S)

    a, qT, k, vT = _front0(x, row(mix_norm[0]), w_in[0].astype(BF16), pool_w[0].astype(BF16),
                         row(pool_scale[0]), rc, rs1, rs2)
    lamv = jnp.stack([lam_q1[0], lam_k1[0], lam_q2[0], lam_k2[0]]).astype(F32)
    lambda_init = 0.8 - 0.6 * math.exp(-0.3 * 0)
    b = _attn(qT, k, vT, lamv, subln[0].reshape(-1, 1), lambda_init)
    h = _mid0(x.reshape(B * S, D), a.reshape(B * S, POOL_WIDTH), b.reshape(B * S, ATTN_WIDTH),
              w_out[0].astype(BF16), row(mlp_norm[0]), w_up[0].astype(BF16),
              w_down[0].astype(BF16))

    h = _conv1(h.reshape(B, S, D), row(mix_norm[1]), conv_pw1_w[0].astype(BF16),
               row(conv_pw1_b[0]), conv_dw_w[0], row(conv_dw_b[0]), row(conv_ln_g[0]),
               row(conv_ln_b[0]), conv_pw2_w[0].astype(BF16), row(conv_pw2_b[0]))
    out = _mlp1(h.reshape(B * S, D), row(mlp_norm[1]), w_up[1].astype(BF16),
                w_down[1].astype(BF16), row(final_norm))
    return out.reshape(B, S, D)
```

```python
import functools
import math

import jax
import jax.numpy as jnp
from jax import lax
from jax.experimental import pallas as pl
from jax.experimental.pallas import tpu as pltpu

F32 = jnp.float32
BF16 = jnp.bfloat16

D_MODEL = 1024
POOL_WIDTH = 512
POOL_WINDOWS = (2, 4, 8, 16)
POOL_GROUP = 128
ATTN_WIDTH = 512
DA_HEAD_V = 128
DA_HEADS = 4
DA_HEAD_QK = 64
ROT_DIM = 16
ROPE_THETA = 500000.0
IN_WIDTH = 2048
CONV_WIDTH = 1024
CONV_KERNEL = 31
D_FF = 4096
RMS_EPS = 1e-6
LN_EPS = 1e-5
SUBLN_EPS = 1e-5

V7X_VMEM_LIMIT_BYTES = 56 * 1024 * 1024
SUBLANES = 8
LANES = 128

TOK_TILE = 512
ATT_TILE = 512
FF_CHUNK = 1024
FRONT_ROW_BLOCKS = 2
LAYER1_ROW_BLOCKS = 2
POOL_HALO = 16
CONV_HALO = 32
CONV_ROWS = 32


def _rms(x, g, eps):
    ms = jnp.mean(x * x, axis=-1, keepdims=True)
    return x * lax.rsqrt(ms + eps) * g


def _const_spec(shape):
    nd = len(shape)
    return pl.BlockSpec(shape, lambda *_: (0,) * nd, pipeline_mode=pl.Buffered(1))


def _mlp(h, g, wup_ref, wdn_ref):
    hn = _rms(h, g, RMS_EPS).astype(BF16)
    acc = None
    for c in range(D_FF // FF_CHUNK):
        cols = slice(c * FF_CHUNK, (c + 1) * FF_CHUNK)
        up = jnp.dot(hn, wup_ref[:, cols], preferred_element_type=F32)
        r = jnp.maximum(up, 0.0)
        d = jnp.dot((r * r).astype(BF16), wdn_ref[cols, :], preferred_element_type=F32)
        acc = d if acc is None else acc + d
    return acc


def _front0_kernel(x_ref, g_ref, win_ref, pw_ref, ps_ref, rc_ref, rs1_ref, rs2_ref,
                   a_ref, qT_ref, k_ref, vT_ref, ubuf):
    t = pl.program_id(1)
    T = x_ref.shape[0]
    R = T // FRONT_ROW_BLOCKS
    scale = DA_HEAD_QK ** -0.5 * math.log2(math.e)

    @pl.when(t == 0)
    def _():
        ubuf[0:POOL_HALO, :] = jnp.zeros((POOL_HALO, POOL_WIDTH), F32)

    zs = []
    for rb in range(FRONT_ROW_BLOCKS):
        hn = _rms(x_ref[rb * R:(rb + 1) * R, :], g_ref[...], RMS_EPS).astype(BF16)
        zs.append(jnp.dot(hn, win_ref[...], preferred_element_type=F32))

    for rb, z in enumerate(zs):
        rows = slice(rb * R, (rb + 1) * R)
        u0 = POOL_HALO + rb * R
        ubuf[u0:u0 + R, :] = z[:, :POOL_WIDTH]
        pos = t * T + rb * R + lax.broadcasted_iota(jnp.int32, (R, 1), 0)
        for g, w in enumerate(POOL_WINDOWS):
            lanes = slice(g * POOL_GROUP, (g + 1) * POOL_GROUP)
            ug = ubuf[u0:u0 + R, lanes]
            acc = ug
            for j in range(1, w):
                acc = acc + ubuf[u0 - j:u0 - j + R, lanes]
            cnt = jnp.minimum(pos + 1, w).astype(F32)
            pooled = acc / cnt - ug
            ag = jnp.dot(pooled.astype(BF16), pw_ref[g], preferred_element_type=F32)
            a_ref[rows, lanes] = (ag * ps_ref[:, lanes]).astype(BF16)

        rc = rc_ref[rows, :]
        rs1 = rs1_ref[rows, :]
        rs2 = rs2_ref[rows, :]

        def rope(xh):
            return (xh * rc + pltpu.roll(xh, LANES - ROT_DIM // 2, 1) * rs1
                    + pltpu.roll(xh, ROT_DIM // 2, 1) * rs2)

        for h in range(DA_HEADS):
            lanes = slice(h * LANES, (h + 1) * LANES)
            qh = rope(z[:, POOL_WIDTH + h * LANES: POOL_WIDTH + (h + 1) * LANES]) * scale
            qT_ref[lanes, rows] = qh.T.astype(BF16)
            off = POOL_WIDTH + ATTN_WIDTH
            k_ref[rows, lanes] = rope(z[:, off + h * LANES: off + (h + 1) * LANES]).astype(BF16)
            off = POOL_WIDTH + 2 * ATTN_WIDTH
            vT_ref[lanes, rows] = z[:, off + h * LANES: off + (h + 1) * LANES].T.astype(BF16)
    ubuf[0:POOL_HALO, :] = ubuf[T:T + POOL_HALO, :]


def _front0(x, g, w_in, pool_w, pool_scale, rc, rs1, rs2):
    B, S, D = x.shape
    T = TOK_TILE
    out = jax.ShapeDtypeStruct((B, S, ATTN_WIDTH), BF16)
    outT = jax.ShapeDtypeStruct((B, S // T, ATTN_WIDTH, T), BF16)
    tok = lambda w: pl.BlockSpec((None, T, w), lambda b, t: (b, t, 0))
    tokT = pl.BlockSpec((None, None, ATTN_WIDTH, T), lambda b, t: (b, t, 0, 0))
    rope = pl.BlockSpec((T, LANES), lambda b, t: (t, 0))
    return pl.pallas_call(
        _front0_kernel,
        grid=(B, S // T),
        in_specs=[tok(D), _const_spec((1, D)), _const_spec((D, IN_WIDTH)),
                  _const_spec((len(POOL_WINDOWS), POOL_GROUP, POOL_GROUP)),
                  _const_spec((1, POOL_WIDTH)), rope, rope, rope],
        out_specs=[tok(POOL_WIDTH), tokT, tok(ATTN_WIDTH), tokT],
        out_shape=[out, outT, out, outT],
        scratch_shapes=[pltpu.VMEM((POOL_HALO + T, POOL_WIDTH), F32)],
        compiler_params=pltpu.CompilerParams(
            dimension_semantics=("arbitrary", "arbitrary"),
            vmem_limit_bytes=V7X_VMEM_LIMIT_BYTES),
        name="front0",
    )(x, g, w_in, pool_w, pool_scale, rc, rs1, rs2)


def _attn_kernel(lam_ref, subln_ref, qT_ref, qTn_ref, k_ref, vT_ref, o_ref, acc1, acc2,
                 s1buf, s2buf, qbuf, qnbuf, mxbuf, *, lambda_init):
    i = pl.program_id(2)
    TQ = qT_ref.shape[1]
    acc1[...] = jnp.zeros(acc1.shape, F32)
    acc2[...] = jnp.zeros(acc2.shape, F32)
    comps = ((0, s1buf, acc1), (1, s2buf, acc2))

    def set_q(src_ref, dst):
        qT = src_ref[...]
        feat = lax.broadcasted_iota(jnp.int32, qT.shape, 0)
        zero = jnp.zeros_like(qT)
        dst[0] = jnp.where(feat < DA_HEAD_QK, qT, zero)
        dst[1] = jnp.where(feat >= DA_HEAD_QK, qT, zero)

    set_q(qT_ref, qbuf)

    def scores(j, c, sbuf, masked, q=qbuf):
        sT = jnp.dot(k_ref[j], q[c], preferred_element_type=F32)
        if masked:
            key = lax.broadcasted_iota(jnp.int32, sT.shape, 0)
            qry = lax.broadcasted_iota(jnp.int32, sT.shape, 1)
            sT = jnp.where(key <= qry, sT, -jnp.inf)
        sbuf[...] = sT
        return jnp.max(sT, axis=0, keepdims=True)

    def step(j, carry, nxt):
        if nxt == "next_q":
            set_q(qTn_ref, qnbuf)
        out = []
        for (c, sbuf, acc), (m_old, l_old, mx) in zip(comps, carry):
            m_new = jnp.maximum(m_old, mx)
            alpha = jnp.exp2(m_old - m_new)
            p = jnp.exp2(sbuf[...] - m_new)
            l_new = alpha * l_old + jnp.sum(p, axis=0, keepdims=True)
            p = p.astype(BF16)
            if nxt == "next_q":
                mxbuf[c] = scores(0, c, sbuf, False, q=qnbuf)
            elif nxt is not None:
                mx = scores(j + 1, c, sbuf, nxt)
            acc[...] = alpha * acc[...] + jnp.dot(vT_ref[j], p, preferred_element_type=F32)
            out.append((m_new, l_new, mx))
        return tuple(out)

    @pl.when(i == 0)
    def _():
        for c, sbuf, _ in comps:
            mxbuf[c] = scores(0, c, sbuf, True)

    m0 = jnp.full((1, TQ), -jnp.inf, F32)
    l0 = jnp.zeros((1, TQ), F32)
    carry = ((m0, l0, mxbuf[0]), (m0, l0, mxbuf[1]))
    n_plain = jnp.maximum(i - 1, 0)
    odd = n_plain % 2
    pair = (n_plain // 2) % 2

    def steps(j0, n, cr):
        for d in range(n):
            cr = step(j0 + d, cr, False)
        return cr

    carry = lax.cond(odd == 1, lambda cr: steps(0, 1, cr), lambda cr: cr, carry)
    carry = lax.cond(pair == 1, lambda cr: steps(odd, 2, cr), lambda cr: cr, carry)
    done = odd + 2 * pair
    carry = lax.fori_loop(0, n_plain // 4, lambda jj, cr: steps(done + 4 * jj, 4, cr), carry)
    def finish(cr):
        (_, l1, _), (_, l2, _) = step(i, cr, "next_q")
        lamv = lam_ref[...]
        lam = (jnp.exp(jnp.sum(lamv[0:1] * lamv[1:2], axis=-1, keepdims=True))
               - jnp.exp(jnp.sum(lamv[2:3] * lamv[3:4], axis=-1, keepdims=True))
               + lambda_init)
        o = acc1[...] / l1 - lam * (acc2[...] / l2)
        ms = jnp.mean(o * o, axis=0, keepdims=True)
        o = o * lax.rsqrt(ms + SUBLN_EPS) * subln_ref[...] * (1.0 - lambda_init)
        o_ref[...] = o.T.astype(o_ref.dtype)

    @pl.when(i > 0)
    def _():
        finish(step(i - 1, carry, True))

    @pl.when(i == 0)
    def _():
        finish(carry)


def _attn(qT, k, vT, lamv, subln_col, lambda_init):
    B, NT, _, TQ = qT.shape
    S = NT * TQ
    k = k.reshape(B, NT, TQ, ATTN_WIDTH)
    accs = pltpu.VMEM((DA_HEAD_V, TQ), F32)
    return pl.pallas_call(
        functools.partial(_attn_kernel, lambda_init=lambda_init),
        grid=(B, DA_HEADS, NT),
        in_specs=[_const_spec((4, DA_HEAD_QK)), _const_spec((DA_HEAD_V, 1)),
                  pl.BlockSpec((None, None, LANES, TQ), lambda b, h, i: (b, i, h, 0)),
                  pl.BlockSpec((None, None, LANES, TQ),
                               lambda b, h, i: (b, jnp.minimum(i + 1, NT - 1), h, 0)),
                  pl.BlockSpec((None, NT, TQ, LANES), lambda b, h, i: (b, 0, 0, h)),
                  pl.BlockSpec((None, NT, LANES, TQ), lambda b, h, i: (b, 0, h, 0))],
        out_specs=pl.BlockSpec((None, TQ, LANES), lambda b, h, i: (b, i, h)),
        out_shape=jax.ShapeDtypeStruct((B, S, ATTN_WIDTH), BF16),
        scratch_shapes=[accs, accs, pltpu.VMEM((TQ, TQ), F32), pltpu.VMEM((TQ, TQ), F32),
                        pltpu.VMEM((2, LANES, TQ), BF16), pltpu.VMEM((2, LANES, TQ), BF16),
                        pltpu.VMEM((2, 1, TQ), F32)],
        compiler_params=pltpu.CompilerParams(
            dimension_semantics=("arbitrary", "arbitrary", "arbitrary"),
            vmem_limit_bytes=V7X_VMEM_LIMIT_BYTES),
        name="attn",
    )(lamv, subln_col, qT, qT, k, vT)


def _mid0_kernel(x_ref, a_ref, b_ref, wo_ref, g_ref, wup_ref, wdn_ref, o_ref):
    mix = (jnp.dot(a_ref[...], wo_ref[0:POOL_WIDTH, :], preferred_element_type=F32)
           + jnp.dot(b_ref[...], wo_ref[POOL_WIDTH:, :], preferred_element_type=F32))
    h = x_ref[...] + mix
    o_ref[...] = h + _mlp(h, g_ref[...], wup_ref, wdn_ref)


def _mid0(x, a, b, w_out, g, w_up, w_down):
    N, D = x.shape
    T = TOK_TILE
    tok = lambda w: pl.BlockSpec((T, w), lambda t: (t, 0))
    return pl.pallas_call(
        _mid0_kernel,
        grid=(N // T,),
        in_specs=[tok(D), tok(POOL_WIDTH), tok(ATTN_WIDTH), _const_spec((D, D)),
                  _const_spec((1, D)), _const_spec((D, D_FF)), _const_spec((D_FF, D))],
        out_specs=tok(D),
        out_shape=jax.ShapeDtypeStruct((N, D), F32),
        compiler_params=pltpu.CompilerParams(
            dimension_semantics=("arbitrary",),
            vmem_limit_bytes=V7X_VMEM_LIMIT_BYTES),
        name="mid0",
    )(x, a, b, w_out, g, w_up, w_down)


def _conv_rows(gbuf, dww_ref, ybuf, base):
    lead = CONV_HALO - (CONV_KERNEL - 1)
    nwin = CONV_ROWS + CONV_HALO
    for lt in range(CONV_WIDTH // LANES):
        lanes = slice(lt * LANES, (lt + 1) * LANES)
        win = gbuf[pl.ds(base, nwin), lanes]
        acc = None
        for r in range(SUBLANES):
            shifted = win if r == 0 else pltpu.roll(win, nwin - r, 0)
            for a in range(CONV_HALO // SUBLANES + 1):
                k = SUBLANES * a + r - lead
                if 0 <= k < CONV_KERNEL:
                    term = dww_ref[k:k + 1, lanes] * shifted[SUBLANES * a:SUBLANES * a + CONV_ROWS, :]
                    acc = term if acc is None else acc + term
        ybuf[pl.ds(base, CONV_ROWS), lanes] = acc


def _conv1_kernel(x_ref, gmix_ref, pw1w_ref, pw1b_ref, dww_ref, dwb_ref, lng_ref, lnb_ref,
                  pw2w_ref, pw2b_ref, o_ref, gbuf, ybuf):
    t = pl.program_id(1)
    T = x_ref.shape[0]

    @pl.when(t == 0)
    def _():
        gbuf[0:CONV_HALO, :] = jnp.zeros((CONV_HALO, CONV_WIDTH), F32)

    R = T // LAYER1_ROW_BLOCKS
    row_blocks = [slice(rb * R, (rb + 1) * R) for rb in range(LAYER1_ROW_BLOCKS)]
    gates = []
    for rows in row_blocks:
        hn = _rms(x_ref[rows, :], gmix_ref[...], RMS_EPS).astype(BF16)
        gates.append(jnp.dot(hn, pw1w_ref[...], preferred_element_type=F32) + pw1b_ref[...])
    for rb, a in enumerate(gates):
        g0 = CONV_HALO + rb * R
        gbuf[g0:g0 + R, :] = a[:, :CONV_WIDTH] * jax.nn.sigmoid(a[:, CONV_WIDTH:])

    def conv_step(c, carry):
        _conv_rows(gbuf, dww_ref, ybuf, pl.multiple_of(c * CONV_ROWS, CONV_ROWS))
        return carry

    lax.fori_loop(0, T // CONV_ROWS, conv_step, 0)
    gbuf[0:CONV_HALO, :] = gbuf[T:T + CONV_HALO, :]

    outs = []
    for rows in row_blocks:
        y = ybuf[rows, :] + dwb_ref[...]
        mu = jnp.mean(y, axis=-1, keepdims=True)
        yc = y - mu
        var = jnp.mean(yc * yc, axis=-1, keepdims=True)
        y = yc * lax.rsqrt(var + LN_EPS) * lng_ref[...] + lnb_ref[...]
        y = y * jax.nn.sigmoid(y)
        outs.append(jnp.dot(y.astype(BF16), pw2w_ref[...], preferred_element_type=F32)
                    + pw2b_ref[...])
    for rows, c in zip(row_blocks, outs):
        o_ref[rows, :] = x_ref[rows, :] + c


def _conv1(x, gmix, pw1w, pw1b, dww, dwb, lng, lnb, pw2w, pw2b):
    B, S, D = x.shape
    T = TOK_TILE
    assert T % CONV_ROWS == 0
    tok = pl.BlockSpec((None, T, D), lambda b, t: (b, t, 0))
    vec = lambda w: _const_spec((1, w))
    return pl.pallas_call(
        _conv1_kernel,
        grid=(B, S // T),
        in_specs=[tok, vec(D), _const_spec((D, 2 * CONV_WIDTH)), vec(2 * CONV_WIDTH),
                  _const_spec((CONV_KERNEL, CONV_WIDTH)), vec(CONV_WIDTH), vec(CONV_WIDTH),
                  vec(CONV_WIDTH), _const_spec((CONV_WIDTH, D)), vec(D)],
        out_specs=tok,
        out_shape=jax.ShapeDtypeStruct((B, S, D), F32),
        scratch_shapes=[pltpu.VMEM((CONV_HALO + T, CONV_WIDTH), F32),
                        pltpu.VMEM((T, CONV_WIDTH), F32)],
        compiler_params=pltpu.CompilerParams(
            dimension_semantics=("arbitrary", "arbitrary"),
            vmem_limit_bytes=V7X_VMEM_LIMIT_BYTES),
        name="conv1",
    )(x, gmix, pw1w, pw1b, dww, dwb, lng, lnb, pw2w, pw2b)


def _mlp1_kernel(h_ref, g_ref, wup_ref, wdn_ref, gfin_ref, o_ref):
    h = h_ref[...]
    o_ref[...] = _rms(h + _mlp(h, g_ref[...], wup_ref, wdn_ref), gfin_ref[...], RMS_EPS)


def _mlp1(h, g, w_up, w_down, gfin):
    N, D = h.shape
    T = TOK_TILE
    tok = pl.BlockSpec((T, D), lambda t: (t, 0))
    return pl.pallas_call(
        _mlp1_kernel,
        grid=(N // T,),
        in_specs=[tok, _const_spec((1, D)), _const_spec((D, D_FF)), _const_spec((D_FF, D)),
                  _const_spec((1, D))],
        out_specs=tok,
        out_shape=jax.ShapeDtypeStruct((N, D), F32),
        compiler_params=pltpu.CompilerParams(
            dimension_semantics=("arbitrary",),
            vmem_limit_bytes=V7X_VMEM_LIMIT_BYTES),
        name="mlp1",
    )(h, g, w_up, w_down, gfin)


def _rope_lane_tables(S):
    half = ROT_DIM // 2
    pos = jnp.arange(S, dtype=F32)
    inv_freq = ROPE_THETA ** (-jnp.arange(0, ROT_DIM, 2, dtype=F32) / ROT_DIM)
    ang = pos[:, None] * inv_freq[None, :]
    cos, sin = jnp.cos(ang), jnp.sin(ang)
    rest = DA_HEAD_QK - ROT_DIM
    one = jnp.ones((S, rest), F32)
    zero = jnp.zeros((S, rest), F32)
    zh = jnp.zeros((S, half), F32)
    rc = jnp.concatenate([cos, cos, one], axis=-1)
    rs1 = jnp.concatenate([-sin, zh, zero], axis=-1)
    rs2 = jnp.concatenate([zh, sin, zero], axis=-1)
    rep = LANES // DA_HEAD_QK
    return tuple(jnp.tile(tbl, (1, rep)) for tbl in (rc, rs1, rs2))


def kernel(x, mix_norm, mlp_norm, w_up, w_down, final_norm, w_in, pool_w, pool_scale,
           lam_q1, lam_k1, lam_q2, lam_k2, subln, w_out, conv_pw1_w, conv_pw1_b,
           conv_dw_w, conv_dw_b, conv_ln_g, conv_ln_b, conv_pw2_w, conv_pw2_b):
    B, S, D = x.shape
    row = lambda v: v.reshape(1, -1)
    rc, rs1, rs2 = _rope_lane_tables(S)

    a, qT, k, vT = _front0(x, row(mix_norm[0]), w_in[0].astype(BF16), pool_w[0].astype(BF16),
                         row(pool_scale[0]), rc, rs1, rs2)
    lamv = jnp.stack([lam_q1[0], lam_k1[0], lam_q2[0], lam_k2[0]]).astype(F32)
    lambda_init = 0.8 - 0.6 * math.exp(-0.3 * 0)
    b = _attn(qT, k, vT, lamv, subln[0].reshape(-1, 1), lambda_init)
    h = _mid0(x.reshape(B * S, D), a.reshape(B * S, POOL_WIDTH), b.reshape(B * S, ATTN_WIDTH),
              w_out[0].astype(BF16), row(mlp_norm[0]), w_up[0].astype(BF16),
              w_down[0].astype(BF16))

    h = _conv1(h.reshape(B, S, D), row(mix_norm[1]), conv_pw1_w[0].astype(BF16),
               row(conv_pw1_b[0]), conv_dw_w[0], row(conv_dw_b[0]), row(conv_ln_g[0]),
               row(conv_ln_b[0]), conv_pw2_w[0].astype(BF16), row(conv_pw2_b[0]))
    out = _mlp1(h.reshape(B * S, D), row(mlp_norm[1]), w_up[1].astype(BF16),
                w_down[1].astype(BF16), row(final_norm))
    return out.reshape(B, S, D)
```

```python
import functools
import math

import jax
import jax.numpy as jnp
from jax import lax
from jax.experimental import pallas as pl
from jax.experimental.pallas import tpu as pltpu

F32 = jnp.float32
BF16 = jnp.bfloat16

D_MODEL = 1024
POOL_WIDTH = 512
POOL_WINDOWS = (2, 4, 8, 16)
POOL_GROUP = 128
ATTN_WIDTH = 512
DA_HEAD_V = 128
DA_HEADS = 4
DA_HEAD_QK = 64
ROT_DIM = 16
ROPE_THETA = 500000.0
IN_WIDTH = 2048
CONV_WIDTH = 1024
CONV_KERNEL = 31
D_FF = 4096
RMS_EPS = 1e-6
LN_EPS = 1e-5
SUBLN_EPS = 1e-5

V7X_VMEM_LIMIT_BYTES = 56 * 1024 * 1024
SUBLANES = 8
LANES = 128

TOK_TILE = 512
ATT_TILE = 512
FF_CHUNK = 1024
FRONT_ROW_BLOCKS = 2
LAYER1_ROW_BLOCKS = 2
POOL_HALO = 16
CONV_HALO = 32
CONV_ROWS = 64


def _rms(x, g, eps):
    ms = jnp.mean(x * x, axis=-1, keepdims=True)
    return x * lax.rsqrt(ms + eps) * g


def _const_spec(shape):
    nd = len(shape)
    return pl.BlockSpec(shape, lambda *_: (0,) * nd, pipeline_mode=pl.Buffered(1))


def _mlp(h, g, wup_ref, wdn_ref):
    hn = _rms(h, g, RMS_EPS).astype(BF16)
    acc = None
    for c in range(D_FF // FF_CHUNK):
        cols = slice(c * FF_CHUNK, (c + 1) * FF_CHUNK)
        up = jnp.dot(hn, wup_ref[:, cols], preferred_element_type=F32)
        r = jnp.maximum(up, 0.0)
        d = jnp.dot((r * r).astype(BF16), wdn_ref[cols, :], preferred_element_type=F32)
        acc = d if acc is None else acc + d
    return acc


def _front0_kernel(x_ref, g_ref, win_ref, pw_ref, ps_ref, rc_ref, rs1_ref, rs2_ref,
                   a_ref, qT_ref, k_ref, vT_ref, ubuf):
    t = pl.program_id(1)
    T = x_ref.shape[0]
    R = T // FRONT_ROW_BLOCKS
    scale = DA_HEAD_QK ** -0.5 * math.log2(math.e)

    @pl.when(t == 0)
    def _():
        ubuf[0:POOL_HALO, :] = jnp.zeros((POOL_HALO, POOL_WIDTH), F32)

    zs = []
    for rb in range(FRONT_ROW_BLOCKS):
        hn = _rms(x_ref[rb * R:(rb + 1) * R, :], g_ref[...], RMS_EPS).astype(BF16)
        zs.append(jnp.dot(hn, win_ref[...], preferred_element_type=F32))

    for rb, z in enumerate(zs):
        rows = slice(rb * R, (rb + 1) * R)
        u0 = POOL_HALO + rb * R
        ubuf[u0:u0 + R, :] = z[:, :POOL_WIDTH]
        pos = t * T + rb * R + lax.broadcasted_iota(jnp.int32, (R, 1), 0)
        for g, w in enumerate(POOL_WINDOWS):
            lanes = slice(g * POOL_GROUP, (g + 1) * POOL_GROUP)
            ug = ubuf[u0:u0 + R, lanes]
            acc = ug
            for j in range(1, w):
                acc = acc + ubuf[u0 - j:u0 - j + R, lanes]
            cnt = jnp.minimum(pos + 1, w).astype(F32)
            pooled = acc / cnt - ug
            ag = jnp.dot(pooled.astype(BF16), pw_ref[g], preferred_element_type=F32)
            a_ref[rows, lanes] = (ag * ps_ref[:, lanes]).astype(BF16)

        rc = rc_ref[rows, :]
        rs1 = rs1_ref[rows, :]
        rs2 = rs2_ref[rows, :]

        def rope(xh):
            return (xh * rc + pltpu.roll(xh, LANES - ROT_DIM // 2, 1) * rs1
                    + pltpu.roll(xh, ROT_DIM // 2, 1) * rs2)

        for h in range(DA_HEADS):
            lanes = slice(h * LANES, (h + 1) * LANES)
            qh = rope(z[:, POOL_WIDTH + h * LANES: POOL_WIDTH + (h + 1) * LANES]) * scale
            qT_ref[lanes, rows] = qh.T.astype(BF16)
            off = POOL_WIDTH + ATTN_WIDTH
            k_ref[rows, lanes] = rope(z[:, off + h * LANES: off + (h + 1) * LANES]).astype(BF16)
            off = POOL_WIDTH + 2 * ATTN_WIDTH
            vT_ref[lanes, rows] = z[:, off + h * LANES: off + (h + 1) * LANES].T.astype(BF16)
    ubuf[0:POOL_HALO, :] = ubuf[T:T + POOL_HALO, :]


def _front0(x, g, w_in, pool_w, pool_scale, rc, rs1, rs2):
    B, S, D = x.shape
    T = TOK_TILE
    out = jax.ShapeDtypeStruct((B, S, ATTN_WIDTH), BF16)
    outT = jax.ShapeDtypeStruct((B, S // T, ATTN_WIDTH, T), BF16)
    tok = lambda w: pl.BlockSpec((None, T, w), lambda b, t: (b, t, 0))
    tokT = pl.BlockSpec((None, None, ATTN_WIDTH, T), lambda b, t: (b, t, 0, 0))
    rope = pl.BlockSpec((T, LANES), lambda b, t: (t, 0))
    return pl.pallas_call(
        _front0_kernel,
        grid=(B, S // T),
        in_specs=[tok(D), _const_spec((1, D)), _const_spec((D, IN_WIDTH)),
                  _const_spec((len(POOL_WINDOWS), POOL_GROUP, POOL_GROUP)),
                  _const_spec((1, POOL_WIDTH)), rope, rope, rope],
        out_specs=[tok(POOL_WIDTH), tokT, tok(ATTN_WIDTH), tokT],
        out_shape=[out, outT, out, outT],
        scratch_shapes=[pltpu.VMEM((POOL_HALO + T, POOL_WIDTH), F32)],
        compiler_params=pltpu.CompilerParams(
            dimension_semantics=("arbitrary", "arbitrary"),
            vmem_limit_bytes=V7X_VMEM_LIMIT_BYTES),
        name="front0",
    )(x, g, w_in, pool_w, pool_scale, rc, rs1, rs2)


def _attn_kernel(lam_ref, subln_ref, qT_ref, qTn_ref, k_ref, vT_ref, o_ref, acc1, acc2,
                 s1buf, s2buf, qbuf, qnbuf, mxbuf, *, lambda_init):
    i = pl.program_id(2)
    TQ = qT_ref.shape[1]
    acc1[...] = jnp.zeros(acc1.shape, F32)
    acc2[...] = jnp.zeros(acc2.shape, F32)
    comps = ((0, s1buf, acc1), (1, s2buf, acc2))

    def set_q(src_ref, dst):
        qT = src_ref[...]
        feat = lax.broadcasted_iota(jnp.int32, qT.shape, 0)
        zero = jnp.zeros_like(qT)
        dst[0] = jnp.where(feat < DA_HEAD_QK, qT, zero)
        dst[1] = jnp.where(feat >= DA_HEAD_QK, qT, zero)

    set_q(qT_ref, qbuf)

    def scores(j, c, sbuf, masked, q=qbuf):
        sT = jnp.dot(k_ref[j], q[c], preferred_element_type=F32)
        if masked:
            key = lax.broadcasted_iota(jnp.int32, sT.shape, 0)
            qry = lax.broadcasted_iota(jnp.int32, sT.shape, 1)
            sT = jnp.where(key <= qry, sT, -jnp.inf)
        sbuf[...] = sT
        return jnp.max(sT, axis=0, keepdims=True)

    def step(j, carry, nxt):
        if nxt == "next_q":
            set_q(qTn_ref, qnbuf)
        out = []
        for (c, sbuf, acc), (m_old, l_old, mx) in zip(comps, carry):
            m_new = jnp.maximum(m_old, mx)
            alpha = jnp.exp2(m_old - m_new)
            p = jnp.exp2(sbuf[...] - m_new)
            l_new = alpha * l_old + jnp.sum(p, axis=0, keepdims=True)
            p = p.astype(BF16)
            if nxt == "next_q":
                mxbuf[c] = scores(0, c, sbuf, False, q=qnbuf)
            elif nxt is not None:
                mx = scores(j + 1, c, sbuf, nxt)
            acc[...] = alpha * acc[...] + jnp.dot(vT_ref[j], p, preferred_element_type=F32)
            out.append((m_new, l_new, mx))
        return tuple(out)

    @pl.when(i == 0)
    def _():
        for c, sbuf, _ in comps:
            mxbuf[c] = scores(0, c, sbuf, True)

    m0 = jnp.full((1, TQ), -jnp.inf, F32)
    l0 = jnp.zeros((1, TQ), F32)
    carry = ((m0, l0, mxbuf[0]), (m0, l0, mxbuf[1]))
    n_plain = jnp.maximum(i - 1, 0)
    odd = n_plain % 2
    pair = (n_plain // 2) % 2

    def steps(j0, n, cr):
        for d in range(n):
            cr = step(j0 + d, cr, False)
        return cr

    carry = lax.cond(odd == 1, lambda cr: steps(0, 1, cr), lambda cr: cr, carry)
    carry = lax.cond(pair == 1, lambda cr: steps(odd, 2, cr), lambda cr: cr, carry)
    done = odd + 2 * pair
    carry = lax.fori_loop(0, n_plain // 4, lambda jj, cr: steps(done + 4 * jj, 4, cr), carry)
    def finish(cr):
        (_, l1, _), (_, l2, _) = step(i, cr, "next_q")
        lamv = lam_ref[...]
        lam = (jnp.exp(jnp.sum(lamv[0:1] * lamv[1:2], axis=-1, keepdims=True))
               - jnp.exp(jnp.sum(lamv[2:3] * lamv[3:4], axis=-1, keepdims=True))
               + lambda_init)
        o = acc1[...] / l1 - lam * (acc2[...] / l2)
        ms = jnp.mean(o * o, axis=0, keepdims=True)
        o = o * lax.rsqrt(ms + SUBLN_EPS) * subln_ref[...] * (1.0 - lambda_init)
        o_ref[...] = o.T.astype(o_ref.dtype)

    @pl.when(i > 0)
    def _():
        finish(step(i - 1, carry, True))

    @pl.when(i == 0)
    def _():
        finish(carry)


def _attn(qT, k, vT, lamv, subln_col, lambda_init):
    B, NT, _, TQ = qT.shape
    S = NT * TQ
    k = k.reshape(B, NT, TQ, ATTN_WIDTH)
    accs = pltpu.VMEM((DA_HEAD_V, TQ), F32)
    return pl.pallas_call(
        functools.partial(_attn_kernel, lambda_init=lambda_init),
        grid=(B, DA_HEADS, NT),
        in_specs=[_const_spec((4, DA_HEAD_QK)), _const_spec((DA_HEAD_V, 1)),
                  pl.BlockSpec((None, None, LANES, TQ), lambda b, h, i: (b, i, h, 0)),
                  pl.BlockSpec((None, None, LANES, TQ),
                               lambda b, h, i: (b, jnp.minimum(i + 1, NT - 1), h, 0)),
                  pl.BlockSpec((None, NT, TQ, LANES), lambda b, h, i: (b, 0, 0, h)),
                  pl.BlockSpec((None, NT, LANES, TQ), lambda b, h, i: (b, 0, h, 0))],
        out_specs=pl.BlockSpec((None, TQ, LANES), lambda b, h, i: (b, i, h)),
        out_shape=jax.ShapeDtypeStruct((B, S, ATTN_WIDTH), BF16),
        scratch_shapes=[accs, accs, pltpu.VMEM((TQ, TQ), F32), pltpu.VMEM((TQ, TQ), F32),
                        pltpu.VMEM((2, LANES, TQ), BF16), pltpu.VMEM((2, LANES, TQ), BF16),
                        pltpu.VMEM((2, 1, TQ), F32)],
        compiler_params=pltpu.CompilerParams(
            dimension_semantics=("arbitrary", "arbitrary", "arbitrary"),
            vmem_limit_bytes=V7X_VMEM_LIMIT_BYTES),
        name="attn",
    )(lamv, subln_col, qT, qT, k, vT)


def _mid0_kernel(x_ref, a_ref, b_ref, wo_ref, g_ref, wup_ref, wdn_ref, o_ref):
    mix = (jnp.dot(a_ref[...], wo_ref[0:POOL_WIDTH, :], preferred_element_type=F32)
           + jnp.dot(b_ref[...], wo_ref[POOL_WIDTH:, :], preferred_element_type=F32))
    h = x_ref[...] + mix
    o_ref[...] = h + _mlp(h, g_ref[...], wup_ref, wdn_ref)


def _mid0(x, a, b, w_out, g, w_up, w_down):
    N, D = x.shape
    T = TOK_TILE
    tok = lambda w: pl.BlockSpec((T, w), lambda t: (t, 0))
    return pl.pallas_call(
        _mid0_kernel,
        grid=(N // T,),
        in_specs=[tok(D), tok(POOL_WIDTH), tok(ATTN_WIDTH), _const_spec((D, D)),
                  _const_spec((1, D)), _const_spec((D, D_FF)), _const_spec((D_FF, D))],
        out_specs=tok(D),
        out_shape=jax.ShapeDtypeStruct((N, D), F32),
        compiler_params=pltpu.CompilerParams(
            dimension_semantics=("arbitrary",),
            vmem_limit_bytes=V7X_VMEM_LIMIT_BYTES),
        name="mid0",
    )(x, a, b, w_out, g, w_up, w_down)


def _conv_rows(gbuf, dww_ref, ybuf, base):
    lead = CONV_HALO - (CONV_KERNEL - 1)
    nwin = CONV_ROWS + CONV_HALO
    for lt in range(CONV_WIDTH // LANES):
        lanes = slice(lt * LANES, (lt + 1) * LANES)
        win = gbuf[pl.ds(base, nwin), lanes]
        acc = None
        for r in range(SUBLANES):
            shifted = win if r == 0 else pltpu.roll(win, nwin - r, 0)
            for a in range(CONV_HALO // SUBLANES + 1):
                k = SUBLANES * a + r - lead
                if 0 <= k < CONV_KERNEL:
                    term = dww_ref[k:k + 1, lanes] * shifted[SUBLANES * a:SUBLANES * a + CONV_ROWS, :]
                    acc = term if acc is None else acc + term
        ybuf[pl.ds(base, CONV_ROWS), lanes] = acc


def _conv1_kernel(x_ref, gmix_ref, pw1w_ref, pw1b_ref, dww_ref, dwb_ref, lng_ref, lnb_ref,
                  pw2w_ref, pw2b_ref, o_ref, gbuf, ybuf):
    t = pl.program_id(1)
    T = x_ref.shape[0]

    @pl.when(t == 0)
    def _():
        gbuf[0:CONV_HALO, :] = jnp.zeros((CONV_HALO, CONV_WIDTH), F32)

    R = T // LAYER1_ROW_BLOCKS
    row_blocks = [slice(rb * R, (rb + 1) * R) for rb in range(LAYER1_ROW_BLOCKS)]
    gates = []
    for rows in row_blocks:
        hn = _rms(x_ref[rows, :], gmix_ref[...], RMS_EPS).astype(BF16)
        gates.append(jnp.dot(hn, pw1w_ref[...], preferred_element_type=F32) + pw1b_ref[...])
    for rb, a in enumerate(gates):
        g0 = CONV_HALO + rb * R
        gbuf[g0:g0 + R, :] = a[:, :CONV_WIDTH] * jax.nn.sigmoid(a[:, CONV_WIDTH:])

    def conv_step(c, carry):
        _conv_rows(gbuf, dww_ref, ybuf, pl.multiple_of(c * CONV_ROWS, CONV_ROWS))
        return carry

    lax.fori_loop(0, T // CONV_ROWS, conv_step, 0)
    gbuf[0:CONV_HALO, :] = gbuf[T:T + CONV_HALO, :]

    outs = []
    for rows in row_blocks:
        y = ybuf[rows, :] + dwb_ref[...]
        mu = jnp.mean(y, axis=-1, keepdims=True)
        yc = y - mu
        var = jnp.mean(yc * yc, axis=-1, keepdims=True)
        y = yc * lax.rsqrt(var + LN_EPS) * lng_ref[...] + lnb_ref[...]
        y = y * jax.nn.sigmoid(y)
        outs.append(jnp.dot(y.astype(BF16), pw2w_ref[...], preferred_element_type=F32)
                    + pw2b_ref[...])
    for rows, c in zip(row_blocks, outs):
        o_ref[rows, :] = x_ref[rows, :] + c


def _conv1(x, gmix, pw1w, pw1b, dww, dwb, lng, lnb, pw2w, pw2b):
    B, S, D = x.shape
    T = TOK_TILE
    assert T % CONV_ROWS == 0
    tok = pl.BlockSpec((None, T, D), lambda b, t: (b, t, 0))
    vec = lambda w: _const_spec((1, w))
    return pl.pallas_call(
        _conv1_kernel,
        grid=(B, S // T),
        in_specs=[tok, vec(D), _const_spec((D, 2 * CONV_WIDTH)), vec(2 * CONV_WIDTH),
                  _const_spec((CONV_KERNEL, CONV_WIDTH)), vec(CONV_WIDTH), vec(CONV_WIDTH),
                  vec(CONV_WIDTH), _const_spec((CONV_WIDTH, D)), vec(D)],
        out_specs=tok,
        out_shape=jax.ShapeDtypeStruct((B, S, D), F32),
        scratch_shapes=[pltpu.VMEM((CONV_HALO + T, CONV_WIDTH), F32),
                        pltpu.VMEM((T, CONV_WIDTH), F32)],
        compiler_params=pltpu.CompilerParams(
            dimension_semantics=("arbitrary", "arbitrary"),
            vmem_limit_bytes=V7X_VMEM_LIMIT_BYTES),
        name="conv1",
    )(x, gmix, pw1w, pw1b, dww, dwb, lng, lnb, pw2w, pw2b)


def _mlp1_kernel(h_ref, g_ref, wup_ref, wdn_ref, gfin_ref, o_ref):
    h = h_ref[...]
    o_ref[...] = _rms(h + _mlp(h, g_ref[...], wup_ref, wdn_ref), gfin_ref[...], RMS_EPS)


def _mlp1(h, g, w_up, w_down, gfin):
    N, D = h.shape
    T = TOK_TILE
    tok = pl.BlockSpec((T, D), lambda t: (t, 0))
    return pl.pallas_call(
        _mlp1_kernel,
        grid=(N // T,),
        in_specs=[tok, _const_spec((1, D)), _const_spec((D, D_FF)), _const_spec((D_FF, D)),
                  _const_spec((1, D))],
        out_specs=tok,
        out_shape=jax.ShapeDtypeStruct((N, D), F32),
        compiler_params=pltpu.CompilerParams(
            dimension_semantics=("arbitrary",),
            vmem_limit_bytes=V7X_VMEM_LIMIT_BYTES),
        name="mlp1",
    )(h, g, w_up, w_down, gfin)


def _rope_lane_tables(S):
    half = ROT_DIM // 2
    pos = jnp.arange(S, dtype=F32)
    inv_freq = ROPE_THETA ** (-jnp.arange(0, ROT_DIM, 2, dtype=F32) / ROT_DIM)
    ang = pos[:, None] * inv_freq[None, :]
    cos, sin = jnp.cos(ang), jnp.sin(ang)
    rest = DA_HEAD_QK - ROT_DIM
    one = jnp.ones((S, rest), F32)
    zero = jnp.zeros((S, rest), F32)
    zh = jnp.zeros((S, half), F32)
    rc = jnp.concatenate([cos, cos, one], axis=-1)
    rs1 = jnp.concatenate([-sin, zh, zero], axis=-1)
    rs2 = jnp.concatenate([zh, sin, zero], axis=-1)
    rep = LANES // DA_HEAD_QK
    return tuple(jnp.tile(tbl, (1, rep)) for tbl in (rc, rs1, rs2))


def kernel(x, mix_norm, mlp_norm, w_up, w_down, final_norm, w_in, pool_w, pool_scale,
           lam_q1, lam_k1, lam_q2, lam_k2, subln, w_out, conv_pw1_w, conv_pw1_b,
           conv_dw_w, conv_dw_b, conv_ln_g, conv_ln_b, conv_pw2_w, conv_pw2_b):
    B, S, D = x.shape
    row = lambda v: v.reshape(1, -1)
    rc, rs1, rs2 = _rope_lane_tables(S)

    a, qT, k, vT = _front0(x, row(mix_norm[0]), w_in[0].astype(BF16), pool_w[0].astype(BF16),
                         row(pool_scale[0]), rc, rs1, rs2)
    lamv = jnp.stack([lam_q1[0], lam_k1[0], lam_q2[0], lam_k2[0]]).astype(F32)
    lambda_init = 0.8 - 0.6 * math.exp(-0.3 * 0)
    b = _attn(qT, k, vT, lamv, subln[0].reshape(-1, 1), lambda_init)
    h = _mid0(x.reshape(B * S, D), a.reshape(B * S, POOL_WIDTH), b.reshape(B * S, ATTN_WIDTH),
              w_out[0].astype(BF16), row(mlp_norm[0]), w_up[0].astype(BF16),
              w_down[0].astype(BF16))

    h = _conv1(h.reshape(B, S, D), row(mix_norm[1]), conv_pw1_w[0].astype(BF16),
               row(conv_pw1_b[0]), conv_dw_w[0], row(conv_dw_b[0]), row(conv_ln_g[0]),
               row(conv_ln_b[0]), conv_pw2_w[0].astype(BF16), row(conv_pw2_b[0]))
    out = _mlp1(h.reshape(B * S, D), row(mlp_norm[1]), w_up[1].astype(BF16),
                w_down[1].astype(BF16), row(final_norm))
    return out.reshape(B, S, D)
```
